```python
import math
import jax, jax.numpy as jnp
from jax import lax
import numpy as np

D_MODEL = 1024
BATCH = 4
SEQ = 4096
DEPTH = 4

SB_HEADS = 4
SB_DIM = 64
DIFF_HEADS = 4
DIFF_QK_DIM = 64
DIFF_V_DIM = 2 * DIFF_QK_DIM
MLA_HEADS = 4
MLA_NOPE = 64
MLA_ROPE = 32
MLA_V = 64
MLA_Q_RANK = 256
MLA_KV_RANK = 128
MIX_WIDTH = SB_HEADS * SB_DIM + DIFF_HEADS * DIFF_V_DIM + MLA_HEADS * MLA_V
W_IN_SPLITS = (
    SB_HEADS * SB_DIM,
    SB_HEADS * SB_DIM,
    SB_HEADS * SB_DIM,
    DIFF_HEADS * 2 * DIFF_QK_DIM,
    DIFF_HEADS * 2 * DIFF_QK_DIM,
    DIFF_HEADS * DIFF_V_DIM,
    MLA_Q_RANK,
    MLA_KV_RANK,
    MLA_ROPE,
)
W_IN_COLS = sum(W_IN_SPLITS)
D_FF = 2816
PLE_DIM = 256
ROPE_THETA = 500000.0
PARTIAL_ROT = DIFF_QK_DIM // 4
Q_BLOCK = 128
EPS = 1e-6

kernel_name = 'hybrid_sb_diff_mla_macaron_trunk'


def rms_norm(x, gain):
    xf = x.astype(jnp.float32)
    y = xf * lax.rsqrt(jnp.mean(xf * xf, axis=-1, keepdims=True) + EPS)
    return (y * gain.astype(jnp.float32)).astype(x.dtype)


def rope_tables(positions, rot_dim):
    inv_freq = 1.0 / (ROPE_THETA ** (jnp.arange(0, rot_dim, 2, dtype=jnp.float32) / rot_dim))
    ang = positions.astype(jnp.float32)[..., None] * inv_freq
    return jnp.cos(ang), jnp.sin(ang)


def apply_rope(x, cos, sin):
    half = cos.shape[-1]
    x1 = x[..., :half].astype(jnp.float32)
    x2 = x[..., half:2 * half].astype(jnp.float32)
    rot = jnp.concatenate([x1 * cos - x2 * sin, x2 * cos + x1 * sin], axis=-1).astype(x.dtype)
    return jnp.concatenate([rot, x[..., 2 * half:]], axis=-1)


def swiglu(u, w_gu, w_down):
    g, v = jnp.split(u @ w_gu, 2, axis=-1)
    return (jax.nn.silu(g) * v) @ w_down


def to_blocks(t):
    b, s = t.shape[:2]
    return jnp.swapaxes(t.reshape((b, s // Q_BLOCK, Q_BLOCK) + t.shape[2:]), 0, 1)


def from_blocks(t):
    nb, b, qb = t.shape[:3]
    return jnp.swapaxes(t, 0, 1).reshape((b, nb * qb) + t.shape[3:])


def block_sweep(fn, q):
    nb = q.shape[1] // Q_BLOCK
    starts = jnp.arange(nb, dtype=jnp.int32) * Q_BLOCK
    return from_blocks(lax.map(fn, (to_blocks(q), starts)))


def causal_mask(start, seq, strict):
    qpos = start + jnp.arange(Q_BLOCK, dtype=jnp.int32)
    kpos = jnp.arange(seq, dtype=jnp.int32)
    if strict:
        return kpos[None, :] < qpos[:, None]
    return kpos[None, :] <= qpos[:, None]


def stick_breaking_attention(q, k, v):
    scale = q.shape[-1] ** -0.5

    def blk(args):
        qb, start = args
        z = jnp.einsum('bqhd,bkhd->bhqk', qb, k).astype(jnp.float32) * scale
        mask = causal_mask(start, k.shape[1], True)
        log_beta = jax.nn.log_sigmoid(z)
        log_keep = jnp.where(mask, jax.nn.log_sigmoid(-z), 0.0)
        log_between = lax.cumsum(log_keep, axis=3, reverse=True) - log_keep
        a = jnp.where(mask, jnp.exp(log_beta + log_between), 0.0)
        return jnp.einsum('bhqk,bkhd->bqhd', a.astype(v.dtype), v)

    return block_sweep(blk, q)


def differential_attention(q, k, v, lam):
    scale = q.shape[-1] ** -0.5

    def blk(args):
        qb, start = args
        s = jnp.einsum('bqhcd,bkhcd->bchqk', qb, k).astype(jnp.float32) * scale
        mask = causal_mask(start, k.shape[1], False)
        pr = jax.nn.softmax(jnp.where(mask, s, -jnp.inf), axis=-1)
        attn = pr[:, 0] - lam * pr[:, 1]
        return jnp.einsum('bhqk,bkhe->bqhe', attn.astype(v.dtype), v)

    return block_sweep(blk, q)


def causal_softmax_attention(q, k, v):
    scale = q.shape[-1] ** -0.5

    def blk(args):
        qb, start = args
        s = jnp.einsum('bqhd,bkhd->bhqk', qb, k).astype(jnp.float32) * scale
        mask = causal_mask(start, k.shape[1], False)
        pr = jax.nn.softmax(jnp.where(mask, s, -jnp.inf), axis=-1)
        return jnp.einsum('bhqk,bkhd->bqhd', pr.astype(v.dtype), v)

    return block_sweep(blk, q)


def token_mixing(u, rope_diff, rope_mla, layer, w_in, mla_q_norm, mla_w_uq, mla_kv_norm,
                 mla_w_ukv, lq1, lk1, lq2, lk2, diff_subln, w_out):
    b, s, _ = u.shape
    offsets = list(np.cumsum(W_IN_SPLITS)[:-1])
    (sb_q, sb_k, sb_v, df_q, df_k, df_v, c_q, c_kv, k_pe) = jnp.split(u @ w_in, offsets, axis=-1)

    shp = (b, s, SB_HEADS, SB_DIM)
    out_a = stick_breaking_attention(sb_q.reshape(shp), sb_k.reshape(shp), sb_v.reshape(shp))

    cos_d, sin_d = rope_diff
    qk_shp = (b, s, DIFF_HEADS, 2, DIFF_QK_DIM)
    dq = apply_rope(df_q.reshape(qk_shp), cos_d, sin_d)
    dk = apply_rope(df_k.reshape(qk_shp), cos_d, sin_d)
    dv = df_v.reshape(b, s, DIFF_HEADS, DIFF_V_DIM)
    lambda_init = 0.8 - 0.6 * math.exp(-0.3 * layer)
    f32 = jnp.float32
    lam = (jnp.exp(jnp.sum(lq1.astype(f32) * lk1.astype(f32)))
           - jnp.exp(jnp.sum(lq2.astype(f32) * lk2.astype(f32))) + lambda_init)
    out_b = differential_attention(dq, dk, dv, lam)
    out_b = rms_norm(out_b, diff_subln) * (1.0 - lambda_init)

    cos_m, sin_m = rope_mla
    q_c = (rms_norm(c_q, mla_q_norm) @ mla_w_uq).reshape(b, s, MLA_HEADS, MLA_NOPE + MLA_ROPE)
    q_c = jnp.concatenate([q_c[..., :MLA_NOPE], apply_rope(q_c[..., MLA_NOPE:], cos_m, sin_m)], axis=-1)
    kv = (rms_norm(c_kv, mla_kv_norm) @ mla_w_ukv).reshape(b, s, MLA_HEADS, MLA_NOPE + MLA_V)
    k_rot = apply_rope(k_pe[:, :, None, :], cos_m, sin_m)
    k_c = jnp.concatenate([kv[..., :MLA_NOPE],
                           jnp.broadcast_to(k_rot, (b, s, MLA_HEADS, MLA_ROPE))], axis=-1)
    out_c = causal_softmax_attention(q_c, k_c, kv[..., MLA_NOPE:])

    mixed = jnp.concatenate([out_a.reshape(b, s, -1), out_b.reshape(b, s, -1),
                             out_c.reshape(b, s, -1)], axis=-1)
    return mixed @ w_out


def setup_inputs(seed: int = 0) -> dict:
    key = jax.random.key(seed)
    ks = jax.random.split(key, 32)
    f32 = jnp.float32

    def w(k, shape, fan_in):
        return jax.random.normal(k, shape, f32) * (fan_in ** -0.5)

    def gain(k, shape):
        return 1.0 + 0.02 * jax.random.normal(k, shape, f32)

    L, D = DEPTH, D_MODEL
    return {
        'x': jax.random.normal(ks[0], (BATCH, SEQ, D), f32),
        'p': jax.random.normal(ks[1], (DEPTH, BATCH, SEQ, PLE_DIM), f32),
        'positions': jnp.broadcast_to(jnp.arange(SEQ, dtype=jnp.int32), (BATCH, SEQ)),
        'norm_ffn1': gain(ks[2], (L, D)),
        'w_ffn1_gu': w(ks[3], (L, D, 2 * D_FF), D),
        'w_ffn1_down': w(ks[4], (L, D_FF, D), D_FF),
        'norm_mix': gain(ks[5], (L, D)),
        'w_in': w(ks[6], (L, D, W_IN_COLS), D),
        'mla_q_norm': gain(ks[7], (L, MLA_Q_RANK)),
        'mla_w_uq': w(ks[8], (L, MLA_Q_RANK, MLA_HEADS * (MLA_NOPE + MLA_ROPE)), MLA_Q_RANK),
        'mla_kv_norm': gain(ks[9], (L, MLA_KV_RANK)),
        'mla_w_ukv': w(ks[10], (L, MLA_KV_RANK, MLA_HEADS * (MLA_NOPE + MLA_V)), MLA_KV_RANK),
        'diff_lambda_q1': 0.1 * jax.random.normal(ks[11], (L, DIFF_QK_DIM), f32),
        'diff_lambda_k1': 0.1 * jax.random.normal(ks[12], (L, DIFF_QK_DIM), f32),
        'diff_lambda_q2': 0.1 * jax.random.normal(ks[13], (L, DIFF_QK_DIM), f32),
        'diff_lambda_k2': 0.1 * jax.random.normal(ks[14], (L, DIFF_QK_DIM), f32),
        'diff_subln': gain(ks[15], (L, DIFF_V_DIM)),
        'w_out': w(ks[16], (L, MIX_WIDTH, D), MIX_WIDTH),
        'norm_ffn2': gain(ks[17], (L, D)),
        'w_ffn2_gu': w(ks[18], (L, D, 2 * D_FF), D),
        'w_ffn2_down': w(ks[19], (L, D_FF, D), D_FF),
        'norm_ple': gain(ks[20], (L, D)),
        'w_ple_gate': w(ks[21], (L, D, D), D),
        'w_ple_proj': w(ks[22], (L, PLE_DIM, D), PLE_DIM),
        'norm_final': gain(ks[23], (D,)),
    }


def reference(x, p, positions, norm_ffn1, w_ffn1_gu, w_ffn1_down, norm_mix, w_in, mla_q_norm,
              mla_w_uq, mla_kv_norm, mla_w_ukv, diff_lambda_q1, diff_lambda_k1, diff_lambda_q2,
              diff_lambda_k2, diff_subln, w_out, norm_ffn2, w_ffn2_gu, w_ffn2_down, norm_ple,
              w_ple_gate, w_ple_proj, norm_final):
    cos_d, sin_d = rope_tables(positions, PARTIAL_ROT)
    rope_diff = (cos_d[:, :, None, None, :], sin_d[:, :, None, None, :])
    cos_m, sin_m = rope_tables(positions, MLA_ROPE)
    rope_mla = (cos_m[:, :, None, :], sin_m[:, :, None, :])

    h = x
    for i in range(DEPTH):
        h = h + 0.5 * swiglu(rms_norm(h, norm_ffn1[i]), w_ffn1_gu[i], w_ffn1_down[i])
        h = h + token_mixing(rms_norm(h, norm_mix[i]), rope_diff, rope_mla, i, w_in[i],
                             mla_q_norm[i], mla_w_uq[i], mla_kv_norm[i], mla_w_ukv[i],
                             diff_lambda_q1[i], diff_lambda_k1[i], diff_lambda_q2[i],
                             diff_lambda_k2[i], diff_subln[i], w_out[i])
        h = h + 0.5 * swiglu(rms_norm(h, norm_ffn2[i]), w_ffn2_gu[i], w_ffn2_down[i])
        gate = jax.nn.sigmoid(rms_norm(h, norm_ple[i]) @ w_ple_gate[i])
        h = h + (p[i] @ w_ple_proj[i]) * gate
    return rms_norm(h, norm_final)
```

```python
import functools
import math

import jax
import jax.numpy as jnp
from jax import lax
from jax.experimental import pallas as pl
from jax.experimental.pallas import tpu as pltpu

F32 = jnp.float32
BF16 = jnp.bfloat16

D_MODEL = 1024
SB_HEADS, SB_DIM = 4, 64
DIFF_HEADS, DIFF_QK, DIFF_V = 4, 64, 128
MLA_HEADS, MLA_NOPE, MLA_ROPE, MLA_V = 4, 64, 32, 64
MLA_Q_RANK, MLA_KV_RANK = 256, 128
D_FF = 2816
PLE_DIM = 256
ROPE_THETA = 500000.0
DIFF_ROT_HALF = 8
MLA_ROT_HALF = 16
EPS = 1e-6
LOG2E = 1.4426950408889634
MASK_VALUE = -1e30

LANES = 128
MXU_DIM = 256
VMEM_LIMIT_BYTES = 56 * 1024 * 1024

ROW_TILE = 512
FF_TILE = D_FF // 2
ATT_TILE = MXU_DIM

_SPLITS = (256, 256, 256, 512, 512, 512, MLA_Q_RANK, MLA_KV_RANK, MLA_ROPE)
_OFFS = tuple(int(sum(_SPLITS[:i])) for i in range(len(_SPLITS) + 1))
N_NAT = 256 + 512 + MLA_Q_RANK + MLA_KV_RANK + LANES
N_TR = 256 + 256 + 512 + 512


def _params(sem):
    return pltpu.CompilerParams(dimension_semantics=sem, vmem_limit_bytes=VMEM_LIMIT_BYTES)


def _rms_norm(x, gain):
    ms = jnp.mean(x * x, axis=-1, keepdims=True)
    return x * lax.rsqrt(ms + EPS) * gain


def _dot(a, b):
    return jnp.dot(a, b, preferred_element_type=F32)


def _dot_nt(a, b):
    return lax.dot_general(a, b, (((1,), (1,)), ((), ())), preferred_element_type=F32)


def _ffn_kernel(h_ref, gain_ref, wg_ref, wv_ref, wd_ref, o_ref, xn_ref, acc_ref, *, n_ff):
    j = pl.program_id(1)

    @pl.when(j == 0)
    def _():
        xn_ref[...] = _rms_norm(h_ref[...], gain_ref[...]).astype(BF16)
        acc_ref[...] = jnp.zeros_like(acc_ref)

    xn = xn_ref[...]
    g = _dot(xn, wg_ref[...])
    v = _dot(xn, wv_ref[...])
    a = (g * jax.nn.sigmoid(g) * v).astype(BF16)
    acc_ref[...] += _dot(a, wd_ref[...])

    @pl.when(j == n_ff - 1)
    def _():
        o_ref[...] = h_ref[...] + 0.5 * acc_ref[...]


def _ffn(h, gain, w_gu, w_down, layer):
    t = h.shape[0]
    n_ff = D_FF // FF_TILE
    grid = (t // ROW_TILE, n_ff)
    return pl.pallas_call(
        functools.partial(_ffn_kernel, n_ff=n_ff),
        grid=grid,
        in_specs=[
            pl.BlockSpec((ROW_TILE, D_MODEL), lambda i, j: (i, 0)),
            pl.BlockSpec((None, 1, D_MODEL), lambda i, j: (layer, 0, 0)),
            pl.BlockSpec((None, D_MODEL, FF_TILE), lambda i, j: (layer, 0, j)),
            pl.BlockSpec((None, D_MODEL, FF_TILE), lambda i, j: (layer, 0, j + n_ff)),
            pl.BlockSpec((None, FF_TILE, D_MODEL), lambda i, j: (layer, j, 0)),
        ],
        out_specs=pl.BlockSpec((ROW_TILE, D_MODEL), lambda i, j: (i, 0)),
        out_shape=jax.ShapeDtypeStruct((t, D_MODEL), F32),
        scratch_shapes=[pltpu.VMEM((ROW_TILE, D_MODEL), BF16), pltpu.VMEM((ROW_TILE, D_MODEL), F32)],
        compiler_params=_params(("parallel", "arbitrary")),
        name="ffn",
    )(h, gain, w_gu, w_gu, w_down)


def _rope_lanes(x, c, sp, sm, shift):
    return x * c + pltpu.roll(x, shift, 1) * sp + pltpu.roll(x, LANES - shift, 1) * sm


def _rope_rows(x, base, half, cos, sin):
    x1 = x[base:base + half]
    x2 = x[base + half:base + 2 * half]
    return x1 * cos - x2 * sin, x2 * cos + x1 * sin


def _proj_kernel(h_ref, gain_ref, wnat_ref, wtr_ref, qn_ref, kvn_ref, wuq_ref, wuk_ref, wuv_ref,
                 tabn_ref, tabt_ref,
                 qta_ref, ka_ref, vta_ref, qtb_ref, kb_ref, vtb_ref, qtc_ref, kc_ref, vtc_ref):
    n_sub = ROW_TILE // ATT_TILE
    u = _rms_norm(h_ref[...], gain_ref[...]).astype(BF16)
    nat = _dot(u, wnat_ref[...])
    tr = _dot_nt(wtr_ref[...], u)

    tabn = tabn_ref[...]
    tabt = tabt_ref[0]
    cos_d, sin_d = tabt[0:8], tabt[8:16]
    cos_m, sin_m = tabt[16:32], tabt[32:48]

    def put_rows(ref, val):
        for c in range(n_sub):
            ref[0, c] = val[c * ATT_TILE:(c + 1) * ATT_TILE].astype(ref.dtype)

    def put_cols(ref, val):
        for c in range(n_sub):
            ref[0, c] = val[:, c * ATT_TILE:(c + 1) * ATT_TILE].astype(ref.dtype)

    qta_ref[0] = (tr[0:256] * (SB_DIM ** -0.5)).astype(BF16)
    put_rows(ka_ref, nat[:, 0:256])
    put_cols(vta_ref, tr[256:512])

    qb = tr[512:1024]
    parts = []
    for ch in range(2 * DIFF_HEADS):
        base = ch * DIFF_QK
        r1, r2 = _rope_rows(qb, base, DIFF_ROT_HALF, cos_d, sin_d)
        parts += [r1, r2, qb[base + 2 * DIFF_ROT_HALF:base + DIFF_QK]]
    qtb_ref[0] = (jnp.concatenate(parts, axis=0) * (DIFF_QK ** -0.5 * LOG2E)).astype(BF16)
    kb = nat[:, 256:768]
    c_d, sp_d, sm_d = tabn[:, 0:128], tabn[:, 128:256], tabn[:, 256:384]
    kb = jnp.concatenate(
        [_rope_lanes(kb[:, c * LANES:(c + 1) * LANES], c_d, sp_d, sm_d, DIFF_ROT_HALF) for c in range(4)], axis=1)
    put_rows(kb_ref, kb)
    put_cols(vtb_ref, tr[1024:1536])

    cq = _rms_norm(nat[:, 768:1024], qn_ref[...]).astype(BF16)
    qc = _dot_nt(wuq_ref[...], cq)
    parts = []
    for hd in range(MLA_HEADS):
        base = hd * LANES
        r1, r2 = _rope_rows(qc, base + MLA_NOPE, MLA_ROT_HALF, cos_m, sin_m)
        parts += [qc[base:base + MLA_NOPE], r1, r2, qc[base + MLA_NOPE + MLA_ROPE:base + LANES]]
    qtc_ref[0] = (jnp.concatenate(parts, axis=0) * ((MLA_NOPE + MLA_ROPE) ** -0.5 * LOG2E)).astype(BF16)
    ckv = _rms_norm(nat[:, 1024:1152], kvn_ref[...]).astype(BF16)
    k_nope = _dot(ckv, wuk_ref[...])
    c_m, sp_m, sm_m = tabn[:, 384:512], tabn[:, 512:640], tabn[:, 640:768]
    k_rot = _rope_lanes(nat[:, 1152:1280], c_m, sp_m, sm_m, MLA_ROT_HALF)
    kc = jnp.concatenate([k_nope[:, hd * LANES:(hd + 1) * LANES] + k_rot for hd in range(MLA_HEADS)], axis=1)
    put_rows(kc_ref, kc)
    put_cols(vtc_ref, _dot_nt(wuv_ref[...], ckv))


def _proj(h, gain, wnat, wtr, qn, kvn, wuq, wuk, wuv, tabn, tabt, layer, batch, seq):
    t = h.shape[0]
    nblk = seq // ROW_TILE
    n_sub = ROW_TILE // ATT_TILE
    nkv = seq // ATT_TILE
    grid = (batch, nblk)

    def row_blocked(n):
        return (jax.ShapeDtypeStruct((batch, nkv, ATT_TILE, n), BF16),
                pl.BlockSpec((1, n_sub, ATT_TILE, n), lambda b, i: (b, i, 0, 0)))

    def col_blocked(n):
        return (jax.ShapeDtypeStruct((batch, nkv, n, ATT_TILE), BF16),
                pl.BlockSpec((1, n_sub, n, ATT_TILE), lambda b, i: (b, i, 0, 0)))

    def q_t(n):
        return (jax.ShapeDtypeStruct((batch, n, seq), BF16),
                pl.BlockSpec((1, n, ROW_TILE), lambda b, i: (b, 0, i)))

    outs = [q_t(256), row_blocked(256), col_blocked(256),
            q_t(512), row_blocked(512), col_blocked(512),
            q_t(512), row_blocked(512), col_blocked(256)]
    whole = lambda shape: pl.BlockSpec((None,) + shape, lambda b, i: (layer,) + (0,) * len(shape))
    return pl.pallas_call(
        _proj_kernel,
        grid=grid,
        in_specs=[
            pl.BlockSpec((ROW_TILE, D_MODEL), lambda b, i: (b * nblk + i, 0)),
            whole((1, D_MODEL)),
            whole((D_MODEL, N_NAT)),
            whole((N_TR, D_MODEL)),
            whole((1, MLA_Q_RANK)),
            whole((1, MLA_KV_RANK)),
            whole((4 * LANES, MLA_Q_RANK)),
            whole((MLA_KV_RANK, 4 * LANES)),
            whole((MLA_HEADS * MLA_V, MLA_KV_RANK)),
            pl.BlockSpec((ROW_TILE, 6 * LANES), lambda b, i: (b * nblk + i, 0)),
            pl.BlockSpec((1, 48, ROW_TILE), lambda b, i: (b, 0, i)),
        ],
        out_specs=[o[1] for o in outs],
        out_shape=[o[0] for o in outs],
        compiler_params=_params(("parallel", "parallel")),
        name="proj",
    )(h, gain, wnat, wtr, qn, kvn, wuq, wuk, wuv, tabn, tabt)


def _row_mask(n_rows, lo, hi, n_cols):
    r = lax.broadcasted_iota(jnp.int32, (n_rows, n_cols), 0)
    return jnp.where((r >= lo) & (r < hi), 1.0, 0.0).astype(BF16)


def _softmax_block(k_blk, q_m, vt_blk, m_ref, l_ref, acc_ref, s_idx, diag):
    ta = ATT_TILE
    s = _dot(k_blk, q_m)
    if diag:
        kk = lax.broadcasted_iota(jnp.int32, (ta, ta), 0)
        qq = lax.broadcasted_iota(jnp.int32, (ta, ta), 1)
        s = jnp.where(kk <= qq, s, MASK_VALUE)
    m_old = m_ref[s_idx]
    m_new = jnp.maximum(m_old, jnp.max(s, axis=0, keepdims=True))
    alpha = jnp.exp2(m_old - m_new)
    p = jnp.exp2(s - m_new)
    l_ref[s_idx] = alpha * l_ref[s_idx] + jnp.sum(p, axis=0, keepdims=True)
    acc_ref[s_idx] = alpha * acc_ref[s_idx] + _dot(vt_blk, p.astype(BF16))
    m_ref[s_idx] = m_new


def _diff_attn_kernel(lq1_ref, lk1_ref, lq2_ref, lk2_ref, subln_ref, qt_ref, k_ref, vt_ref, o_ref,
                      m_ref, l_ref, acc_ref, *, lambda_init):
    i = pl.program_id(1)
    ta = ATT_TILE
    m_ref[...] = jnp.full_like(m_ref, MASK_VALUE)
    l_ref[...] = jnp.zeros_like(l_ref)
    acc_ref[...] = jnp.zeros_like(acc_ref)

    qt = qt_ref[0]
    q_masked = []
    for hd in range(DIFF_HEADS):
        qh = qt[hd * LANES:(hd + 1) * LANES]
        q_masked.append((qh * _row_mask(LANES, 0, DIFF_QK, ta), qh * _row_mask(LANES, DIFF_QK, LANES, ta)))

    def block(j, diag):
        k_blk = k_ref[0, j]
        vt_blk = vt_ref[0, j]
        for hd in range(DIFF_HEADS):
            kh = k_blk[:, hd * LANES:(hd + 1) * LANES]
            vh = vt_blk[hd * DIFF_V:(hd + 1) * DIFF_V]
            for comp in range(2):
                _softmax_block(kh, q_masked[hd][comp], vh, m_ref, l_ref, acc_ref, 2 * hd + comp, diag)

    def body(j, carry):
        block(j, False)
        return carry

    lax.fori_loop(0, i, body, 0)
    block(i, True)

    lam = (jnp.exp(jnp.sum(lq1_ref[...] * lk1_ref[...], axis=-1, keepdims=True))
           - jnp.exp(jnp.sum(lq2_ref[...] * lk2_ref[...], axis=-1, keepdims=True)) + lambda_init)
    outs = []
    for hd in range(DIFF_HEADS):
        o1 = acc_ref[2 * hd] / l_ref[2 * hd]
        o2 = acc_ref[2 * hd + 1] / l_ref[2 * hd + 1]
        o = o1 - lam * o2
        ms = jnp.mean(o * o, axis=0, keepdims=True)
        outs.append(o * lax.rsqrt(ms + EPS))
    o_nat = jnp.concatenate(outs, axis=0).T
    o_ref[0] = (o_nat * subln_ref[...] * (1.0 - lambda_init)).astype(o_ref.dtype)


def _diff_attn(lq1, lk1, lq2, lk2, subln4, qt, k, vt, layer, batch, seq):
    nq = seq // ATT_TILE
    lambda_init = 0.8 - 0.6 * math.exp(-0.3 * layer)
    vec = lambda n: pl.BlockSpec((None, 1, n), lambda b, i: (layer, 0, 0))
    return pl.pallas_call(
        functools.partial(_diff_attn_kernel, lambda_init=lambda_init),
        grid=(batch, nq),
        in_specs=[
            vec(DIFF_QK), vec(DIFF_QK), vec(DIFF_QK), vec(DIFF_QK), vec(DIFF_HEADS * DIFF_V),
            pl.BlockSpec((1, 512, ATT_TILE), lambda b, i: (b, 0, i)),
            pl.BlockSpec((1, nq, ATT_TILE, 512), lambda b, i: (b, 0, 0, 0)),
            pl.BlockSpec((1, nq, 512, ATT_TILE), lambda b, i: (b, 0, 0, 0)),
        ],
        out_specs=pl.BlockSpec((1, ATT_TILE, 512), lambda b, i: (b, i, 0)),
        out_shape=jax.ShapeDtypeStruct((batch, seq, 512), BF16),
        scratch_shapes=[pltpu.VMEM((8, 1, ATT_TILE), F32), pltpu.VMEM((8, 1, ATT_TILE), F32),
                        pltpu.VMEM((8, DIFF_V, ATT_TILE), F32)],
        compiler_params=_params(("parallel", "arbitrary")),
        name="diff_attn",
    )(lq1, lk1, lq2, lk2, subln4, qt, k, vt)


def _mla_attn_kernel(qt_ref, k_ref, vt_ref, o_ref, m_ref, l_ref, acc_ref):
    i = pl.program_id(1)
    m_ref[...] = jnp.full_like(m_ref, MASK_VALUE)
    l_ref[...] = jnp.zeros_like(l_ref)
    acc_ref[...] = jnp.zeros_like(acc_ref)
    qt = qt_ref[0]
    q_heads = [qt[hd * LANES:(hd + 1) * LANES] for hd in range(MLA_HEADS)]

    def block(j, diag):
        k_blk = k_ref[0, j]
        vt_blk = vt_ref[0, j]
        for hd in range(MLA_HEADS):
            _softmax_block(k_blk[:, hd * LANES:(hd + 1) * LANES], q_heads[hd],
                           vt_blk[hd * MLA_V:(hd + 1) * MLA_V], m_ref, l_ref, acc_ref, hd, diag)

    def body(j, carry):
        block(j, False)
        return carry

    lax.fori_loop(0, i, body, 0)
    block(i, True)
    outs = [acc_ref[hd] / l_ref[hd] for hd in range(MLA_HEADS)]
    o_ref[0] = jnp.concatenate(outs, axis=0).T.astype(o_ref.dtype)


def _mla_attn(qt, k, vt, batch, seq):
    nq = seq // ATT_TILE
    return pl.pallas_call(
        _mla_attn_kernel,
        grid=(batch, nq),
        in_specs=[
            pl.BlockSpec((1, 512, ATT_TILE), lambda b, i: (b, 0, i)),
            pl.BlockSpec((1, nq, ATT_TILE, 512), lambda b, i: (b, 0, 0, 0)),
            pl.BlockSpec((1, nq, 256, ATT_TILE), lambda b, i: (b, 0, 0, 0)),
        ],
        out_specs=pl.BlockSpec((1, ATT_TILE, 256), lambda b, i: (b, i, 0)),
        out_shape=jax.ShapeDtypeStruct((batch, seq, 256), BF16),
        scratch_shapes=[pltpu.VMEM((4, 1, ATT_TILE), F32), pltpu.VMEM((4, 1, ATT_TILE), F32),
                        pltpu.VMEM((4, MLA_V, ATT_TILE), F32)],
        compiler_params=_params(("parallel", "arbitrary")),
        name="mla_attn",
    )(qt, k, vt)


def _sb_attn_kernel(qt_ref, k_ref, vt_ref, o_ref, c_ref, acc_ref):
    i = pl.program_id(1)
    ta = ATT_TILE
    c_ref[...] = jnp.zeros_like(c_ref)
    acc_ref[...] = jnp.zeros_like(acc_ref)
    qt = qt_ref[0]
    q_masked = []
    for hd in range(SB_HEADS):
        qh = qt[(hd // 2) * LANES:(hd // 2 + 1) * LANES]
        lo = (hd % 2) * SB_DIM
        q_masked.append(qh * _row_mask(LANES, lo, lo + SB_DIM, ta))
    rr = lax.broadcasted_iota(jnp.int32, (ta, ta), 0)
    cc = lax.broadcasted_iota(jnp.int32, (ta, ta), 1)
    upper = jnp.where(cc > rr, 1.0, 0.0).astype(BF16)

    def block(j, diag):
        k_blk = k_ref[0, j]
        vt_blk = vt_ref[0, j]
        for hd in range(SB_HEADS):
            kh = k_blk[:, (hd // 2) * LANES:(hd // 2 + 1) * LANES]
            z = _dot(kh, q_masked[hd])
            sp = jnp.log(1.0 + jnp.exp(-jnp.abs(z)))
            log_beta = jnp.minimum(z, 0.0) - sp
            log_keep = log_beta - z
            if diag:
                valid = rr < cc
                log_keep = jnp.where(valid, log_keep, 0.0)
            hi = log_keep.astype(BF16)
            lo_part = (log_keep - hi.astype(F32)).astype(BF16)
            between = _dot(upper, hi) + _dot(upper, lo_part)
            a = jnp.exp(log_beta + between + c_ref[hd])
            if diag:
                a = jnp.where(valid, a, 0.0)
            c_ref[hd] = c_ref[hd] + between[0:1] + log_keep[0:1]
            acc_ref[hd] += _dot(vt_blk[hd * SB_DIM:(hd + 1) * SB_DIM], a.astype(BF16))

    block(i, True)

    def body(jj, carry):
        block(i - 1 - jj, False)
        return carry

    lax.fori_loop(0, i, body, 0)
    o_ref[0] = acc_ref[...].reshape(SB_HEADS * SB_DIM, ta).T.astype(o_ref.dtype)


def _sb_attn(qt, k, vt, batch, seq):
    nq = seq // ATT_TILE
    return pl.pallas_call(
        _sb_attn_kernel,
        grid=(batch, nq),
        in_specs=[
            pl.BlockSpec((1, 256, ATT_TILE), lambda b, i: (b, 0, i)),
            pl.BlockSpec((1, nq, ATT_TILE, 256), lambda b, i: (b, 0, 0, 0)),
            pl.BlockSpec((1, nq, 256, ATT_TILE), lambda b, i: (b, 0, 0, 0)),
        ],
        out_specs=pl.BlockSpec((1, ATT_TILE, 256), lambda b, i: (b, i, 0)),
        out_shape=jax.ShapeDtypeStruct((batch, seq, 256), BF16),
        scratch_shapes=[pltpu.VMEM((4, 1, ATT_TILE), F32), pltpu.VMEM((4, SB_DIM, ATT_TILE), F32)],
        compiler_params=_params(("parallel", "arbitrary")),
        name="sb_attn",
    )(qt, k, vt)


def _out_proj_kernel(h_ref, a_ref, b_ref, c_ref, wa_ref, wb_ref, wc_ref, o_ref):
    o_ref[...] = (h_ref[...] + _dot(a_ref[...], wa_ref[...]) + _dot(b_ref[...], wb_ref[...])
                  + _dot(c_ref[...], wc_ref[...]))


def _out_proj(h, oa, ob, oc, w_out, layer):
    t = h.shape[0]
    rows = lambda n: pl.BlockSpec((ROW_TILE, n), lambda i: (i, 0))
    return pl.pallas_call(
        _out_proj_kernel,
        grid=(t // ROW_TILE,),
        in_specs=[
            rows(D_MODEL), rows(256), rows(512), rows(256),
            pl.BlockSpec((None, 256, D_MODEL), lambda i: (layer, 0, 0)),
            pl.BlockSpec((None, 512, D_MODEL), lambda i: (layer, 0, 0)),
            pl.BlockSpec((None, 256, D_MODEL), lambda i: (layer, 0, 0)),
        ],
        out_specs=rows(D_MODEL),
        out_shape=jax.ShapeDtypeStruct((t, D_MODEL), F32),
        compiler_params=_params(("parallel",)),
        name="out_proj",
    )(h, oa, ob, oc, w_out[0], w_out[1], w_out[2])


def _ple_kernel(h_ref, p_ref, gain_ref, wg_ref, wp_ref, fin_ref, o_ref, *, final):
    h = h_ref[...]
    gate = jax.nn.sigmoid(_dot(_rms_norm(h, gain_ref[...]).astype(BF16), wg_ref[...]))
    out = h + _dot(p_ref[0].astype(BF16), wp_ref[...]) * gate
    if final:
        out = _rms_norm(out, fin_ref[...])
    o_ref[...] = out


def _ple(h, p, gain, w_gate, w_proj, norm_final, layer, final):
    t = h.shape[0]
    return pl.pallas_call(
        functools.partial(_ple_kernel, final=final),
        grid=(t // ROW_TILE,),
        in_specs=[
            pl.BlockSpec((ROW_TILE, D_MODEL), lambda i: (i, 0)),
            pl.BlockSpec((1, ROW_TILE, PLE_DIM), lambda i: (layer, i, 0)),
            pl.BlockSpec((None, 1, D_MODEL), lambda i: (layer, 0, 0)),
            pl.BlockSpec((None, D_MODEL, D_MODEL), lambda i: (layer, 0, 0)),
            pl.BlockSpec((None, PLE_DIM, D_MODEL), lambda i: (layer, 0, 0)),
            pl.BlockSpec((1, D_MODEL), lambda i: (0, 0)),
        ],
        out_specs=pl.BlockSpec((ROW_TILE, D_MODEL), lambda i: (i, 0)),
        out_shape=jax.ShapeDtypeStruct((t, D_MODEL), F32),
        compiler_params=_params(("parallel",)),
        name="ple",
    )(h, p, gain, w_gate, w_proj, norm_final)


def _rope_tables(positions):
    b, s = positions.shape
    pos = positions.astype(F32)[..., None]

    def cs(rot_dim):
        inv_freq = 1.0 / (ROPE_THETA ** (jnp.arange(0, rot_dim, 2, dtype=F32) / rot_dim))
        ang = pos * inv_freq
        return jnp.cos(ang), jnp.sin(ang)

    cos_d, sin_d = cs(2 * DIFF_ROT_HALF)
    cos_m, sin_m = cs(2 * MLA_ROT_HALF)
    one = lambda n: jnp.ones((b, s, n), F32)
    zero = lambda n: jnp.zeros((b, s, n), F32)
    c_d = jnp.tile(jnp.concatenate([cos_d, cos_d, one(48)], -1), (1, 1, 2))
    sp_d = jnp.tile(jnp.concatenate([zero(8), sin_d, zero(48)], -1), (1, 1, 2))
    sm_d = jnp.tile(jnp.concatenate([-sin_d, zero(56)], -1), (1, 1, 2))
    c_m = jnp.concatenate([one(64), cos_m, cos_m, one(32)], -1)
    sp_m = jnp.concatenate([zero(80), sin_m, zero(32)], -1)
    sm_m = jnp.concatenate([zero(64), -sin_m, zero(48)], -1)
    tabn = jnp.concatenate([c_d, sp_d, sm_d, c_m, sp_m, sm_m], -1).reshape(b * s, 6 * LANES)
    tabt = jnp.swapaxes(jnp.concatenate([cos_d, sin_d, cos_m, sin_m], -1), 1, 2)
    return tabn, tabt


def _prep_mixer_weights(w_in, mla_w_uq, mla_w_ukv):
    n_l = w_in.shape[0]
    seg = lambda i: w_in[:, :, _OFFS[i]:_OFFS[i + 1]]
    a_q, a_k, a_v, b_q, b_k, b_v, c_q, c_kv, k_pe = (seg(i) for i in range(9))
    zeros = lambda n: jnp.zeros((n_l, D_MODEL, n), w_in.dtype)
    k_pe_pad = jnp.concatenate([zeros(MLA_NOPE), k_pe, zeros(LANES - MLA_NOPE - MLA_ROPE)], -1)
    wnat = jnp.concatenate([a_k, b_k, c_q, c_kv, k_pe_pad], -1).astype(BF16)
    wtr = jnp.swapaxes(jnp.concatenate([a_q, a_v, b_q, b_v], -1), 1, 2).astype(BF16)
    uq = mla_w_uq.reshape(n_l, MLA_Q_RANK, MLA_HEADS, MLA_NOPE + MLA_ROPE)
    uq = jnp.pad(uq, ((0, 0), (0, 0), (0, 0), (0, LANES - MLA_NOPE - MLA_ROPE)))
    wuq = jnp.swapaxes(uq.reshape(n_l, MLA_Q_RANK, MLA_HEADS * LANES), 1, 2).astype(BF16)
    ukv = mla_w_ukv.reshape(n_l, MLA_KV_RANK, MLA_HEADS, MLA_NOPE + MLA_V)
    uk = jnp.pad(ukv[..., :MLA_NOPE], ((0, 0), (0, 0), (0, 0), (0, LANES - MLA_NOPE)))
    wuk = uk.reshape(n_l, MLA_KV_RANK, MLA_HEADS * LANES).astype(BF16)
    wuv = jnp.swapaxes(ukv[..., MLA_NOPE:].reshape(n_l, MLA_KV_RANK, MLA_HEADS * MLA_V), 1, 2).astype(BF16)
    return wnat, wtr, wuq, wuk, wuv


def kernel(x, p, positions, norm_ffn1, w_ffn1_gu, w_ffn1_down, norm_mix, w_in, mla_q_norm, mla_w_uq,
           mla_kv_norm, mla_w_ukv, diff_lambda_q1, diff_lambda_k1, diff_lambda_q2, diff_lambda_k2,
           diff_subln, w_out, norm_ffn2, w_ffn2_gu, w_ffn2_down, norm_ple, w_ple_gate, w_ple_proj,
           norm_final):
    batch, seq, _ = x.shape
    depth = w_in.shape[0]
    assert seq % ROW_TILE == 0 and ROW_TILE % ATT_TILE == 0
    t = batch * seq
    row3 = lambda a: a.reshape(a.shape[0], 1, a.shape[1])

    tabn, tabt = _rope_tables(positions)
    wnat, wtr, wuq, wuk, wuv = _prep_mixer_weights(w_in, mla_w_uq, mla_w_ukv)
    w1_gu, w1_d = w_ffn1_gu.astype(BF16), w_ffn1_down.astype(BF16)
    w2_gu, w2_d = w_ffn2_gu.astype(BF16), w_ffn2_down.astype(BF16)
    w_out_b = w_out.astype(BF16)
    w_out_parts = (w_out_b[:, 0:256], w_out_b[:, 256:768], w_out_b[:, 768:1024])
    w_gate_b, w_pproj_b = w_ple_gate.astype(BF16), w_ple_proj.astype(BF16)
    g_ffn1, g_mix, g_ffn2, g_ple = row3(norm_ffn1), row3(norm_mix), row3(norm_ffn2), row3(norm_ple)
    g_q, g_kv = row3(mla_q_norm), row3(mla_kv_norm)
    lq1, lk1, lq2, lk2 = (row3(a) for a in (diff_lambda_q1, diff_lambda_k1, diff_lambda_q2, diff_lambda_k2))
    subln4 = row3(jnp.tile(diff_subln, (1, DIFF_HEADS)))
    p_rows = p.reshape(depth, t, PLE_DIM)
    fin = norm_final.reshape(1, D_MODEL)

    h = x.reshape(t, D_MODEL)
    for layer in range(depth):
        h = _ffn(h, g_ffn1, w1_gu, w1_d, layer)
        qta, ka, vta, qtb, kb, vtb, qtc, kc, vtc = _proj(
            h, g_mix, wnat, wtr, g_q, g_kv, wuq, wuk, wuv, tabn, tabt, layer, batch, seq)
        oa = _sb_attn(qta, ka, vta, batch, seq)
        ob = _diff_attn(lq1, lk1, lq2, lk2, subln4, qtb, kb, vtb, layer, batch, seq)
        oc = _mla_attn(qtc, kc, vtc, batch, seq)
        h = _out_proj(h, oa.reshape(t, 256), ob.reshape(t, 512), oc.reshape(t, 256), w_out_parts, layer)
        h = _ffn(h, g_ffn2, w2_gu, w2_d, layer)
        h = _ple(h, p_rows, g_ple, w_gate_b, w_pproj_b, fin, layer, final=(layer == depth - 1))
    return h.reshape(batch, seq, D_MODEL)
```

```python
import functools
import math

import jax
import jax.numpy as jnp
from jax import lax
from jax.experimental import pallas as pl
from jax.experimental.pallas import tpu as pltpu

F32 = jnp.float32
BF16 = jnp.bfloat16

D_MODEL = 1024
SB_HEADS, SB_DIM = 4, 64
DIFF_HEADS, DIFF_QK, DIFF_V = 4, 64, 128
MLA_HEADS, MLA_NOPE, MLA_ROPE, MLA_V = 4, 64, 32, 64
MLA_Q_RANK, MLA_KV_RANK = 256, 128
D_FF = 2816
PLE_DIM = 256
ROPE_THETA = 500000.0
DIFF_ROT_HALF = 8
MLA_ROT_HALF = 16
EPS = 1e-6
LOG2E = 1.4426950408889634
MASK_VALUE = -1e30

LANES = 128
MXU_DIM = 256
VMEM_LIMIT_BYTES = 56 * 1024 * 1024

ROW_TILE = 512
FF_TILE = D_FF // 2
ATT_TILE = 2 * MXU_DIM
V_PAD = 16
DIFF_VT = DIFF_V + V_PAD
MLA_VT = MLA_V + V_PAD

_SPLITS = (256, 256, 256, 512, 512, 512, MLA_Q_RANK, MLA_KV_RANK, MLA_ROPE)
_OFFS = tuple(int(sum(_SPLITS[:i])) for i in range(len(_SPLITS) + 1))
N_NAT = 256 + 512 + MLA_Q_RANK + MLA_KV_RANK + LANES
N_TR = 256 + 256 + 512 + 512


def _params(sem):
    return pltpu.CompilerParams(dimension_semantics=sem, vmem_limit_bytes=VMEM_LIMIT_BYTES)


def _rms_norm(x, gain):
    ms = jnp.mean(x * x, axis=-1, keepdims=True)
    return x * lax.rsqrt(ms + EPS) * gain


def _dot(a, b):
    return jnp.dot(a, b, preferred_element_type=F32)


def _dot_nt(a, b):
    return lax.dot_general(a, b, (((1,), (1,)), ((), ())), preferred_element_type=F32)


def _ffn_kernel(h_ref, gain_ref, wg_ref, wv_ref, wd_ref, o_ref, xn_ref, acc_ref, *, n_ff):
    j = pl.program_id(1)

    @pl.when(j == 0)
    def _():
        xn_ref[...] = _rms_norm(h_ref[...], gain_ref[...]).astype(BF16)
        acc_ref[...] = jnp.zeros_like(acc_ref)

    xn = xn_ref[...]
    g = _dot(xn, wg_ref[...])
    v = _dot(xn, wv_ref[...])
    a = (g * jax.nn.sigmoid(g) * v).astype(BF16)
    acc_ref[...] += _dot(a, wd_ref[...])

    @pl.when(j == n_ff - 1)
    def _():
        o_ref[...] = h_ref[...] + 0.5 * acc_ref[...]


def _ffn(h, gain, w_gu, w_down, layer):
    t = h.shape[0]
    n_ff = D_FF // FF_TILE
    grid = (t // ROW_TILE, n_ff)
    return pl.pallas_call(
        functools.partial(_ffn_kernel, n_ff=n_ff),
        grid=grid,
        in_specs=[
            pl.BlockSpec((ROW_TILE, D_MODEL), lambda i, j: (i, 0)),
            pl.BlockSpec((None, 1, D_MODEL), lambda i, j: (layer, 0, 0)),
            pl.BlockSpec((None, D_MODEL, FF_TILE), lambda i, j: (layer, 0, j)),
            pl.BlockSpec((None, D_MODEL, FF_TILE), lambda i, j: (layer, 0, j + n_ff)),
            pl.BlockSpec((None, FF_TILE, D_MODEL), lambda i, j: (layer, j, 0)),
        ],
        out_specs=pl.BlockSpec((ROW_TILE, D_MODEL), lambda i, j: (i, 0)),
        out_shape=jax.ShapeDtypeStruct((t, D_MODEL), F32),
        scratch_shapes=[pltpu.VMEM((ROW_TILE, D_MODEL), BF16), pltpu.VMEM((ROW_TILE, D_MODEL), F32)],
        compiler_params=_params(("parallel", "arbitrary")),
        name="ffn",
    )(h, gain, w_gu, w_gu, w_down)


def _rope_lanes(x, c, sp, sm, shift):
    return x * c + pltpu.roll(x, shift, 1) * sp + pltpu.roll(x, LANES - shift, 1) * sm


def _rope_rows(x, base, half, cos, sin):
    x1 = x[base:base + half]
    x2 = x[base + half:base + 2 * half]
    return x1 * cos - x2 * sin, x2 * cos + x1 * sin


def _proj_kernel(h_ref, gain_ref, wnat_ref, wtr_ref, qn_ref, kvn_ref, wuq_ref, wuk_ref, wuv_ref,
                 tabn_ref, tabt_ref,
                 qta_ref, ka_ref, vta_ref, qtb_ref, kb_ref, vtb_ref, qtc_ref, kc_ref, vtc_ref):
    n_sub = ROW_TILE // ATT_TILE
    u = _rms_norm(h_ref[...], gain_ref[...]).astype(BF16)
    nat = _dot(u, wnat_ref[...])
    tr = _dot_nt(wtr_ref[...], u)

    tabn = tabn_ref[...]
    tabt = tabt_ref[0]
    cos_d, sin_d = tabt[0:8], tabt[8:16]
    cos_m, sin_m = tabt[16:32], tabt[32:48]

    def put_rows(ref, val):
        for c in range(n_sub):
            ref[0, c] = val[c * ATT_TILE:(c + 1) * ATT_TILE].astype(ref.dtype)

    def put_cols(ref, val):
        for c in range(n_sub):
            ref[0, c] = val[:, c * ATT_TILE:(c + 1) * ATT_TILE].astype(ref.dtype)

    tm = h_ref.shape[0]
    ones_row = jnp.where(lax.broadcasted_iota(jnp.int32, (V_PAD, tm), 0) == 0, 1.0, 0.0)

    def with_ones(vt, n_heads, dv):
        parts = []
        for hd in range(n_heads):
            parts += [vt[hd * dv:(hd + 1) * dv], ones_row]
        return jnp.concatenate(parts, axis=0)

    qta_ref[0] = (tr[0:256] * (SB_DIM ** -0.5 * LOG2E)).astype(BF16)
    put_rows(ka_ref, nat[:, 0:256])
    put_cols(vta_ref, tr[256:512])

    qb = tr[512:1024]
    parts = []
    for ch in range(2 * DIFF_HEADS):
        base = ch * DIFF_QK
        r1, r2 = _rope_rows(qb, base, DIFF_ROT_HALF, cos_d, sin_d)
        parts += [r1, r2, qb[base + 2 * DIFF_ROT_HALF:base + DIFF_QK]]
    qtb_ref[0] = (jnp.concatenate(parts, axis=0) * (DIFF_QK ** -0.5 * LOG2E)).astype(BF16)
    kb = nat[:, 256:768]
    c_d, sp_d, sm_d = tabn[:, 0:128], tabn[:, 128:256], tabn[:, 256:384]
    kb = jnp.concatenate(
        [_rope_lanes(kb[:, c * LANES:(c + 1) * LANES], c_d, sp_d, sm_d, DIFF_ROT_HALF) for c in range(4)], axis=1)
    put_rows(kb_ref, kb)
    put_cols(vtb_ref, with_ones(tr[1024:1536], DIFF_HEADS, DIFF_V))

    cq = _rms_norm(nat[:, 768:1024], qn_ref[...]).astype(BF16)
    qc = _dot_nt(wuq_ref[...], cq)
    parts = []
    for hd in range(MLA_HEADS):
        base = hd * LANES
        r1, r2 = _rope_rows(qc, base + MLA_NOPE, MLA_ROT_HALF, cos_m, sin_m)
        parts += [qc[base:base + MLA_NOPE], r1, r2, qc[base + MLA_NOPE + MLA_ROPE:base + LANES]]
    qtc_ref[0] = (jnp.concatenate(parts, axis=0) * ((MLA_NOPE + MLA_ROPE) ** -0.5 * LOG2E)).astype(BF16)
    ckv = _rms_norm(nat[:, 1024:1152], kvn_ref[...]).astype(BF16)
    k_nope = _dot(ckv, wuk_ref[...])
    c_m, sp_m, sm_m = tabn[:, 384:512], tabn[:, 512:640], tabn[:, 640:768]
    k_rot = _rope_lanes(nat[:, 1152:1280], c_m, sp_m, sm_m, MLA_ROT_HALF)
    kc = jnp.concatenate([k_nope[:, hd * LANES:(hd + 1) * LANES] + k_rot for hd in range(MLA_HEADS)], axis=1)
    put_rows(kc_ref, kc)
    put_cols(vtc_ref, with_ones(_dot_nt(wuv_ref[...], ckv), MLA_HEADS, MLA_V))


def _proj(h, gain, wnat, wtr, qn, kvn, wuq, wuk, wuv, tabn, tabt, layer, batch, seq):
    t = h.shape[0]
    nblk = seq // ROW_TILE
    n_sub = ROW_TILE // ATT_TILE
    nkv = seq // ATT_TILE
    grid = (batch, nblk)

    def row_blocked(n):
        return (jax.ShapeDtypeStruct((batch, nkv, ATT_TILE, n), BF16),
                pl.BlockSpec((1, n_sub, ATT_TILE, n), lambda b, i: (b, i, 0, 0)))

    def col_blocked(n):
        return (jax.ShapeDtypeStruct((batch, nkv, n, ATT_TILE), BF16),
                pl.BlockSpec((1, n_sub, n, ATT_TILE), lambda b, i: (b, i, 0, 0)))

    def q_t(n):
        return (jax.ShapeDtypeStruct((batch, n, seq), BF16),
                pl.BlockSpec((1, n, ROW_TILE), lambda b, i: (b, 0, i)))

    outs = [q_t(256), row_blocked(256), col_blocked(256),
            q_t(512), row_blocked(512), col_blocked(DIFF_HEADS * DIFF_VT),
            q_t(512), row_blocked(512), col_blocked(MLA_HEADS * MLA_VT)]
    whole = lambda shape: pl.BlockSpec((None,) + shape, lambda b, i: (layer,) + (0,) * len(shape))
    return pl.pallas_call(
        _proj_kernel,
        grid=grid,
        in_specs=[
            pl.BlockSpec((ROW_TILE, D_MODEL), lambda b, i: (b * nblk + i, 0)),
            whole((1, D_MODEL)),
            whole((D_MODEL, N_NAT)),
            whole((N_TR, D_MODEL)),
            whole((1, MLA_Q_RANK)),
            whole((1, MLA_KV_RANK)),
            whole((4 * LANES, MLA_Q_RANK)),
            whole((MLA_KV_RANK, 4 * LANES)),
            whole((MLA_HEADS * MLA_V, MLA_KV_RANK)),
            pl.BlockSpec((ROW_TILE, 6 * LANES), lambda b, i: (b * nblk + i, 0)),
            pl.BlockSpec((1, 48, ROW_TILE), lambda b, i: (b, 0, i)),
        ],
        out_specs=[o[1] for o in outs],
        out_shape=[o[0] for o in outs],
        compiler_params=_params(("parallel", "parallel")),
        name="proj",
    )(h, gain, wnat, wtr, qn, kvn, wuq, wuk, wuv, tabn, tabt)


def _row_mask(n_rows, lo, hi, n_cols):
    r = lax.broadcasted_iota(jnp.int32, (n_rows, n_cols), 0)
    return jnp.where((r >= lo) & (r < hi), 1.0, 0.0).astype(BF16)


def _causal_softmax_sweep(i, q_heads, k_ref, vt_ref, m_ref, acc_ref, vt_rows):
    ta = ATT_TILE
    n = q_heads[0].shape[1]
    kk = lax.broadcasted_iota(jnp.int32, (ta, n), 0)
    qq = lax.broadcasted_iota(jnp.int32, (ta, n), 1) & (ta - 1)
    visible = kk <= qq

    def block(j, diag):
        k_blk = k_ref[0, j]
        vt_blk = vt_ref[0, j]
        def scores(hd):
            s = _dot(k_blk[:, hd * LANES:(hd + 1) * LANES], q_heads[hd])
            return jnp.where(visible, s, MASK_VALUE) if diag else s

        s = scores(0)
        for hd in range(len(q_heads)):
            m_old = m_ref[hd]
            m_new = jnp.maximum(m_old, jnp.max(s, axis=0, keepdims=True))
            s_next = scores(hd + 1) if hd + 1 < len(q_heads) else None
            alpha = jnp.exp2(m_old - m_new)
            p = jnp.exp2(s - m_new).astype(BF16)
            acc_ref[hd] = alpha * acc_ref[hd] + _dot(vt_blk[hd * vt_rows:(hd + 1) * vt_rows], p)
            m_ref[hd] = m_new
            s = s_next

    def body(j, carry):
        block(j, False)
        return carry

    lax.fori_loop(0, i, body, 0)
    block(i, True)


def _diff_attn_kernel(lq1_ref, lk1_ref, lq2_ref, lk2_ref, subln_ref, qt_ref, k_ref, vt_ref, o_ref,
                      m_ref, acc_ref, *, lambda_init):
    i = pl.program_id(1)
    ta = ATT_TILE
    m_ref[...] = jnp.full_like(m_ref, MASK_VALUE)
    acc_ref[...] = jnp.zeros_like(acc_ref)

    qt = qt_ref[0]
    q_heads = []
    for hd in range(DIFF_HEADS):
        qh = qt[hd * LANES:(hd + 1) * LANES]
        q_heads.append(jnp.concatenate(
            [qh * _row_mask(LANES, 0, DIFF_QK, ta), qh * _row_mask(LANES, DIFF_QK, LANES, ta)], axis=1))

    _causal_softmax_sweep(i, q_heads, k_ref, vt_ref, m_ref, acc_ref, DIFF_VT)

    lam = (jnp.exp(jnp.sum(lq1_ref[...] * lk1_ref[...], axis=-1, keepdims=True))
           - jnp.exp(jnp.sum(lq2_ref[...] * lk2_ref[...], axis=-1, keepdims=True)) + lambda_init)
    outs = []
    for hd in range(DIFF_HEADS):
        acc = acc_ref[hd]
        o12 = acc[0:DIFF_V] / acc[DIFF_V:DIFF_V + 1]
        o = o12[:, 0:ta] - lam * o12[:, ta:2 * ta]
        ms = jnp.mean(o * o, axis=0, keepdims=True)
        outs.append(o * lax.rsqrt(ms + EPS))
    o_nat = jnp.concatenate(outs, axis=0).T
    o_ref[0] = (o_nat * subln_ref[...] * (1.0 - lambda_init)).astype(o_ref.dtype)


def _diff_attn(lq1, lk1, lq2, lk2, subln4, qt, k, vt, layer, batch, seq):
    nq = seq // ATT_TILE
    lambda_init = 0.8 - 0.6 * math.exp(-0.3 * layer)
    vec = lambda n: pl.BlockSpec((None, 1, n), lambda b, i: (layer, 0, 0))
    return pl.pallas_call(
        functools.partial(_diff_attn_kernel, lambda_init=lambda_init),
        grid=(batch, nq),
        in_specs=[
            vec(DIFF_QK), vec(DIFF_QK), vec(DIFF_QK), vec(DIFF_QK), vec(DIFF_HEADS * DIFF_V),
            pl.BlockSpec((1, 512, ATT_TILE), lambda b, i: (b, 0, i)),
            pl.BlockSpec((1, nq, ATT_TILE, 512), lambda b, i: (b, 0, 0, 0)),
            pl.BlockSpec((1, nq, DIFF_HEADS * DIFF_VT, ATT_TILE), lambda b, i: (b, 0, 0, 0)),
        ],
        out_specs=pl.BlockSpec((1, ATT_TILE, 512), lambda b, i: (b, i, 0)),
        out_shape=jax.ShapeDtypeStruct((batch, seq, 512), BF16),
        scratch_shapes=[pltpu.VMEM((DIFF_HEADS, 1, 2 * ATT_TILE), F32),
                        pltpu.VMEM((DIFF_HEADS, DIFF_VT, 2 * ATT_TILE), F32)],
        compiler_params=_params(("parallel", "arbitrary")),
        name="diff_attn",
    )(lq1, lk1, lq2, lk2, subln4, qt, k, vt)


def _mla_attn_kernel(qt_ref, k_ref, vt_ref, o_ref, m_ref, acc_ref):
    i = pl.program_id(1)
    m_ref[...] = jnp.full_like(m_ref, MASK_VALUE)
    acc_ref[...] = jnp.zeros_like(acc_ref)
    qt = qt_ref[0]
    q_heads = [qt[hd * LANES:(hd + 1) * LANES] for hd in range(MLA_HEADS)]
    _causal_softmax_sweep(i, q_heads, k_ref, vt_ref, m_ref, acc_ref, MLA_VT)
    outs = []
    for hd in range(MLA_HEADS):
        acc = acc_ref[hd]
        outs.append(acc[0:MLA_V] / acc[MLA_V:MLA_V + 1])
    o_ref[0] = jnp.concatenate(outs, axis=0).T.astype(o_ref.dtype)


def _mla_attn(qt, k, vt, batch, seq):
    nq = seq // ATT_TILE
    return pl.pallas_call(
        _mla_attn_kernel,
        grid=(batch, nq),
        in_specs=[
            pl.BlockSpec((1, 512, ATT_TILE), lambda b, i: (b, 0, i)),
            pl.BlockSpec((1, nq, ATT_TILE, 512), lambda b, i: (b, 0, 0, 0)),
            pl.BlockSpec((1, nq, MLA_HEADS * MLA_VT, ATT_TILE), lambda b, i: (b, 0, 0, 0)),
        ],
        out_specs=pl.BlockSpec((1, ATT_TILE, 256), lambda b, i: (b, i, 0)),
        out_shape=jax.ShapeDtypeStruct((batch, seq, 256), BF16),
        scratch_shapes=[pltpu.VMEM((MLA_HEADS, 1, ATT_TILE), F32),
                        pltpu.VMEM((MLA_HEADS, MLA_VT, ATT_TILE), F32)],
        compiler_params=_params(("parallel", "arbitrary")),
        name="mla_attn",
    )(qt, k, vt)


def _sb_attn_kernel(qt_ref, k_ref, vt_ref, o_ref, c_ref, acc_ref):
    i = pl.program_id(1)
    ta = ATT_TILE
    c_ref[...] = jnp.zeros_like(c_ref)
    acc_ref[...] = jnp.zeros_like(acc_ref)
    qt = qt_ref[0]
    q_masked = []
    for hd in range(SB_HEADS):
        qh = qt[(hd // 2) * LANES:(hd // 2 + 1) * LANES]
        lo = (hd % 2) * SB_DIM
        q_masked.append(qh * _row_mask(LANES, lo, lo + SB_DIM, ta))
    valid = (lax.broadcasted_iota(jnp.int32, (ta, ta), 0)
             < lax.broadcasted_iota(jnp.int32, (ta, ta), 1))
    half = MXU_DIM
    upper = jnp.where(lax.broadcasted_iota(jnp.int32, (half, half), 1)
                      > lax.broadcasted_iota(jnp.int32, (half, half), 0), 1.0, 0.0).astype(BF16)

    def block(j, diag):
        k_blk = k_ref[0, j]
        vt_blk = vt_ref[0, j]
        for hd in range(SB_HEADS):
            kh = k_blk[:, (hd // 2) * LANES:(hd // 2 + 1) * LANES]
            y = _dot(kh, q_masked[hd])
            sp = jnp.log2(1.0 + jnp.exp2(-jnp.abs(y)))
            log_beta = jnp.minimum(y, 0.0) - sp
            log_keep = log_beta - y
            if diag:
                log_keep = jnp.where(valid, log_keep, 0.0)
            hi = log_keep.astype(BF16)
            lo_part = (log_keep - hi.astype(F32)).astype(BF16)
            later = _dot(upper, hi[half:]) + _dot(upper, lo_part[half:])
            earlier = _dot(upper, hi[:half]) + _dot(upper, lo_part[:half])
            later_total = later[0:1] + log_keep[half:half + 1]
            between = jnp.concatenate([earlier + later_total, later], axis=0)
            a = jnp.exp2(log_beta + between + c_ref[hd])
            if diag:
                a = jnp.where(valid, a, 0.0)
            c_ref[hd] = c_ref[hd] + (earlier[0:1] + log_keep[0:1]) + later_total
            acc_ref[hd] += _dot(vt_blk[hd * SB_DIM:(hd + 1) * SB_DIM], a.astype(BF16))

    block(i, True)

    def body(jj, carry):
        block(i - 1 - jj, False)
        return carry

    lax.fori_loop(0, i, body, 0)
    o_ref[0] = acc_ref[...].reshape(SB_HEADS * SB_DIM, ta).T.astype(o_ref.dtype)


def _sb_attn(qt, k, vt, batch, seq):
    nq = seq // ATT_TILE
    return pl.pallas_call(
        _sb_attn_kernel,
        grid=(batch, nq),
        in_specs=[
            pl.BlockSpec((1, 256, ATT_TILE), lambda b, i: (b, 0, i)),
            pl.BlockSpec((1, nq, ATT_TILE, 256), lambda b, i: (b, 0, 0, 0)),
            pl.BlockSpec((1, nq, 256, ATT_TILE), lambda b, i: (b, 0, 0, 0)),
        ],
        out_specs=pl.BlockSpec((1, ATT_TILE, 256), lambda b, i: (b, i, 0)),
        out_shape=jax.ShapeDtypeStruct((batch, seq, 256), BF16),
        scratch_shapes=[pltpu.VMEM((4, 1, ATT_TILE), F32), pltpu.VMEM((4, SB_DIM, ATT_TILE), F32)],
        compiler_params=_params(("parallel", "arbitrary")),
        name="sb_attn",
    )(qt, k, vt)


def _out_proj_kernel(h_ref, a_ref, b_ref, c_ref, wa_ref, wb_ref, wc_ref, o_ref):
    o_ref[...] = (h_ref[...] + _dot(a_ref[...], wa_ref[...]) + _dot(b_ref[...], wb_ref[...])
                  + _dot(c_ref[...], wc_ref[...]))


def _out_proj(h, oa, ob, oc, w_out, layer):
    t = h.shape[0]
    rows = lambda n: pl.BlockSpec((ROW_TILE, n), lambda i: (i, 0))
    return pl.pallas_call(
        _out_proj_kernel,
        grid=(t // ROW_TILE,),
        in_specs=[
            rows(D_MODEL), rows(256), rows(512), rows(256),
            pl.BlockSpec((None, 256, D_MODEL), lambda i: (layer, 0, 0)),
            pl.BlockSpec((None, 512, D_MODEL), lambda i: (layer, 0, 0)),
            pl.BlockSpec((None, 256, D_MODEL), lambda i: (layer, 0, 0)),
        ],
        out_specs=rows(D_MODEL),
        out_shape=jax.ShapeDtypeStruct((t, D_MODEL), F32),
        compiler_params=_params(("parallel",)),
        name="out_proj",
    )(h, oa, ob, oc, w_out[0], w_out[1], w_out[2])


def _ple_kernel(h_ref, p_ref, gain_ref, wg_ref, wp_ref, fin_ref, o_ref, *, final):
    h = h_ref[...]
    gate = jax.nn.sigmoid(_dot(_rms_norm(h, gain_ref[...]).astype(BF16), wg_ref[...]))
    out = h + _dot(p_ref[0].astype(BF16), wp_ref[...]) * gate
    if final:
        out = _rms_norm(out, fin_ref[...])
    o_ref[...] = out


def _ple(h, p, gain, w_gate, w_proj, norm_final, layer, final):
    t = h.shape[0]
    return pl.pallas_call(
        functools.partial(_ple_kernel, final=final),
        grid=(t // ROW_TILE,),
        in_specs=[
            pl.BlockSpec((ROW_TILE, D_MODEL), lambda i: (i, 0)),
            pl.BlockSpec((1, ROW_TILE, PLE_DIM), lambda i: (layer, i, 0)),
            pl.BlockSpec((None, 1, D_MODEL), lambda i: (layer, 0, 0)),
            pl.BlockSpec((None, D_MODEL, D_MODEL), lambda i: (layer, 0, 0)),
            pl.BlockSpec((None, PLE_DIM, D_MODEL), lambda i: (layer, 0, 0)),
            pl.BlockSpec((1, D_MODEL), lambda i: (0, 0)),
        ],
        out_specs=pl.BlockSpec((ROW_TILE, D_MODEL), lambda i: (i, 0)),
        out_shape=jax.ShapeDtypeStruct((t, D_MODEL), F32),
        compiler_params=_params(("parallel",)),
        name="ple",
    )(h, p, gain, w_gate, w_proj, norm_final)


def _rope_tables(positions):
    b, s = positions.shape
    pos = positions.astype(F32)[..., None]

    def cs(rot_dim):
        inv_freq = 1.0 / (ROPE_THETA ** (jnp.arange(0, rot_dim, 2, dtype=F32) / rot_dim))
        ang = pos * inv_freq
        return jnp.cos(ang), jnp.sin(ang)

    cos_d, sin_d = cs(2 * DIFF_ROT_HALF)
    cos_m, sin_m = cs(2 * MLA_ROT_HALF)
    one = lambda n: jnp.ones((b, s, n), F32)
    zero = lambda n: jnp.zeros((b, s, n), F32)
    c_d = jnp.tile(jnp.concatenate([cos_d, cos_d, one(48)], -1), (1, 1, 2))
    sp_d = jnp.tile(jnp.concatenate([zero(8), sin_d, zero(48)], -1), (1, 1, 2))
    sm_d = jnp.tile(jnp.concatenate([-sin_d, zero(56)], -1), (1, 1, 2))
    c_m = jnp.concatenate([one(64), cos_m, cos_m, one(32)], -1)
    sp_m = jnp.concatenate([zero(80), sin_m, zero(32)], -1)
    sm_m = jnp.concatenate([zero(64), -sin_m, zero(48)], -1)
    tabn = jnp.concatenate([c_d, sp_d, sm_d, c_m, sp_m, sm_m], -1).reshape(b * s, 6 * LANES)
    tabt = jnp.swapaxes(jnp.concatenate([cos_d, sin_d, cos_m, sin_m], -1), 1, 2)
    return tabn, tabt


def _prep_mixer_weights(w_in, mla_w_uq, mla_w_ukv):
    n_l = w_in.shape[0]
    seg = lambda i: w_in[:, :, _OFFS[i]:_OFFS[i + 1]]
    a_q, a_k, a_v, b_q, b_k, b_v, c_q, c_kv, k_pe = (seg(i) for i in range(9))
    zeros = lambda n: jnp.zeros((n_l, D_MODEL, n), w_in.dtype)
    k_pe_pad = jnp.concatenate([zeros(MLA_NOPE), k_pe, zeros(LANES - MLA_NOPE - MLA_ROPE)], -1)
    wnat = jnp.concatenate([a_k, b_k, c_q, c_kv, k_pe_pad], -1).astype(BF16)
    wtr = jnp.swapaxes(jnp.concatenate([a_q, a_v, b_q, b_v], -1), 1, 2).astype(BF16)
    uq = mla_w_uq.reshape(n_l, MLA_Q_RANK, MLA_HEADS, MLA_NOPE + MLA_ROPE)
    uq = jnp.pad(uq, ((0, 0), (0, 0), (0, 0), (0, LANES - MLA_NOPE - MLA_ROPE)))
    wuq = jnp.swapaxes(uq.reshape(n_l, MLA_Q_RANK, MLA_HEADS * LANES), 1, 2).astype(BF16)
    ukv = mla_w_ukv.reshape(n_l, MLA_KV_RANK, MLA_HEADS, MLA_NOPE + MLA_V)
    uk = jnp.pad(ukv[..., :MLA_NOPE], ((0, 0), (0, 0), (0, 0), (0, LANES - MLA_NOPE)))
    wuk = uk.reshape(n_l, MLA_KV_RANK, MLA_HEADS * LANES).astype(BF16)
    wuv = jnp.swapaxes(ukv[..., MLA_NOPE:].reshape(n_l, MLA_KV_RANK, MLA_HEADS * MLA_V), 1, 2).astype(BF16)
    return wnat, wtr, wuq, wuk, wuv


def kernel(x, p, positions, norm_ffn1, w_ffn1_gu, w_ffn1_down, norm_mix, w_in, mla_q_norm, mla_w_uq,
           mla_kv_norm, mla_w_ukv, diff_lambda_q1, diff_lambda_k1, diff_lambda_q2, diff_lambda_k2,
           diff_subln, w_out, norm_ffn2, w_ffn2_gu, w_ffn2_down, norm_ple, w_ple_gate, w_ple_proj,
           norm_final):
    batch, seq, _ = x.shape
    depth = w_in.shape[0]
    assert seq % ROW_TILE == 0 and ROW_TILE % ATT_TILE == 0
    t = batch * seq
    row3 = lambda a: a.reshape(a.shape[0], 1, a.shape[1])

    tabn, tabt = _rope_tables(positions)
    wnat, wtr, wuq, wuk, wuv = _prep_mixer_weights(w_in, mla_w_uq, mla_w_ukv)
    w1_gu, w1_d = w_ffn1_gu.astype(BF16), w_ffn1_down.astype(BF16)
    w2_gu, w2_d = w_ffn2_gu.astype(BF16), w_ffn2_down.astype(BF16)
    w_out_b = w_out.astype(BF16)
    w_out_parts = (w_out_b[:, 0:256], w_out_b[:, 256:768], w_out_b[:, 768:1024])
    w_gate_b, w_pproj_b = w_ple_gate.astype(BF16), w_ple_proj.astype(BF16)
    g_ffn1, g_mix, g_ffn2, g_ple = row3(norm_ffn1), row3(norm_mix), row3(norm_ffn2), row3(norm_ple)
    g_q, g_kv = row3(mla_q_norm), row3(mla_kv_norm)
    lq1, lk1, lq2, lk2 = (row3(a) for a in (diff_lambda_q1, diff_lambda_k1, diff_lambda_q2, diff_lambda_k2))
    subln4 = row3(jnp.tile(diff_subln, (1, DIFF_HEADS)))
    p_rows = p.reshape(depth, t, PLE_DIM)
    fin = norm_final.reshape(1, D_MODEL)

    h = x.reshape(t, D_MODEL)
    for layer in range(depth):
        h = _ffn(h, g_ffn1, w1_gu, w1_d, layer)
        qta, ka, vta, qtb, kb, vtb, qtc, kc, vtc = _proj(
            h, g_mix, wnat, wtr, g_q, g_kv, wuq, wuk, wuv, tabn, tabt, layer, batch, seq)
        oa = _sb_attn(qta, ka, vta, batch, seq)
        ob = _diff_attn(lq1, lk1, lq2, lk2, subln4, qtb, kb, vtb, layer, batch, seq)
        oc = _mla_attn(qtc, kc, vtc, batch, seq)
        h = _out_proj(h, oa.reshape(t, 256), ob.reshape(t, 512), oc.reshape(t, 256), w_out_parts, layer)
        h = _ffn(h, g_ffn2, w2_gu, w2_d, layer)
        h = _ple(h, p_rows, g_ple, w_gate_b, w_pproj_b, fin, layer, final=(layer == depth - 1))
    return h.reshape(batch, seq, D_MODEL)
```

```python
import functools
import math

import jax
import jax.numpy as jnp
from jax import lax
from jax.experimental import pallas as pl
from jax.experimental.pallas import tpu as pltpu

F32 = jnp.float32
BF16 = jnp.bfloat16

D_MODEL = 1024
SB_HEADS, SB_DIM = 4, 64
DIFF_HEADS, DIFF_QK, DIFF_V = 4, 64, 128
MLA_HEADS, MLA_NOPE, MLA_ROPE, MLA_V = 4, 64, 32, 64
MLA_Q_RANK, MLA_KV_RANK = 256, 128
D_FF = 2816
PLE_DIM = 256
ROPE_THETA = 500000.0
DIFF_ROT_HALF = 8
MLA_ROT_HALF = 16
EPS = 1e-6
LOG2E = 1.4426950408889634
MASK_VALUE = -1e30

LANES = 128
MXU_DIM = 256
VMEM_LIMIT_BYTES = 56 * 1024 * 1024

ROW_TILE = 512
FF_TILE = D_FF // 2
ATT_TILE = 2 * MXU_DIM
V_PAD = 16
DIFF_VT = DIFF_V + V_PAD
MLA_VT = MLA_V + V_PAD

_SPLITS = (256, 256, 256, 512, 512, 512, MLA_Q_RANK, MLA_KV_RANK, MLA_ROPE)
_OFFS = tuple(int(sum(_SPLITS[:i])) for i in range(len(_SPLITS) + 1))
N_NAT = 256 + 512 + MLA_Q_RANK + MLA_KV_RANK + LANES
N_TR = 256 + 256 + 512 + 512


def _params(sem):
    return pltpu.CompilerParams(dimension_semantics=sem, vmem_limit_bytes=VMEM_LIMIT_BYTES)


def _rms_norm(x, gain):
    ms = jnp.mean(x * x, axis=-1, keepdims=True)
    return x * lax.rsqrt(ms + EPS) * gain


def _dot(a, b):
    return jnp.dot(a, b, preferred_element_type=F32)


def _dot_nt(a, b):
    return lax.dot_general(a, b, (((1,), (1,)), ((), ())), preferred_element_type=F32)


def _ffn_kernel(h_ref, gain_ref, wg_ref, wv_ref, wd_ref, o_ref, xn_ref, acc_ref, *, n_ff):
    j = pl.program_id(1)

    @pl.when(j == 0)
    def _():
        xn_ref[...] = _rms_norm(h_ref[...], gain_ref[...]).astype(BF16)
        acc_ref[...] = jnp.zeros_like(acc_ref)

    xn = xn_ref[...]
    g = _dot(xn, wg_ref[...])
    v = _dot(xn, wv_ref[...])
    a = (g * jax.nn.sigmoid(g) * v).astype(BF16)
    acc_ref[...] += _dot(a, wd_ref[...])

    @pl.when(j == n_ff - 1)
    def _():
        o_ref[...] = h_ref[...] + 0.5 * acc_ref[...]


def _ffn(h, gain, w_gu, w_down, layer):
    t = h.shape[0]
    n_ff = D_FF // FF_TILE
    grid = (t // ROW_TILE, n_ff)
    return pl.pallas_call(
        functools.partial(_ffn_kernel, n_ff=n_ff),
        grid=grid,
        in_specs=[
            pl.BlockSpec((ROW_TILE, D_MODEL), lambda i, j: (i, 0)),
            pl.BlockSpec((None, 1, D_MODEL), lambda i, j: (layer, 0, 0)),
            pl.BlockSpec((None, D_MODEL, FF_TILE), lambda i, j: (layer, 0, j)),
            pl.BlockSpec((None, D_MODEL, FF_TILE), lambda i, j: (layer, 0, j + n_ff)),
            pl.BlockSpec((None, FF_TILE, D_MODEL), lambda i, j: (layer, j, 0)),
        ],
        out_specs=pl.BlockSpec((ROW_TILE, D_MODEL), lambda i, j: (i, 0)),
        out_shape=jax.ShapeDtypeStruct((t, D_MODEL), F32),
        scratch_shapes=[pltpu.VMEM((ROW_TILE, D_MODEL), BF16), pltpu.VMEM((ROW_TILE, D_MODEL), F32)],
        compiler_params=_params(("parallel", "arbitrary")),
        name="ffn",
    )(h, gain, w_gu, w_gu, w_down)


def _rope_lanes(x, c, sp, sm, shift):
    return x * c + pltpu.roll(x, shift, 1) * sp + pltpu.roll(x, LANES - shift, 1) * sm


def _rope_rows(x, base, half, cos, sin):
    x1 = x[base:base + half]
    x2 = x[base + half:base + 2 * half]
    return x1 * cos - x2 * sin, x2 * cos + x1 * sin


def _proj_kernel(h_ref, gain_ref, wnat_ref, wtr_ref, qn_ref, kvn_ref, wuq_ref, wuk_ref, wuv_ref,
                 tabn_ref, tabt_ref,
                 qta_ref, ka_ref, vta_ref, qtb_ref, kb_ref, vtb_ref, qtc_ref, kc_ref, vtc_ref):
    n_sub = ROW_TILE // ATT_TILE
    u = _rms_norm(h_ref[...], gain_ref[...]).astype(BF16)
    nat = _dot(u, wnat_ref[...])
    tr = _dot_nt(wtr_ref[...], u)

    tabn = tabn_ref[...]
    tabt = tabt_ref[0]
    cos_d, sin_d = tabt[0:8], tabt[8:16]
    cos_m, sin_m = tabt[16:32], tabt[32:48]

    def put_rows(ref, val):
        for c in range(n_sub):
            ref[0, c] = val[c * ATT_TILE:(c + 1) * ATT_TILE].astype(ref.dtype)

    def put_cols(ref, val):
        for c in range(n_sub):
            ref[0, c] = val[:, c * ATT_TILE:(c + 1) * ATT_TILE].astype(ref.dtype)

    tm = h_ref.shape[0]
    ones_row = jnp.where(lax.broadcasted_iota(jnp.int32, (V_PAD, tm), 0) == 0, 1.0, 0.0)

    def with_ones(vt, n_heads, dv):
        parts = []
        for hd in range(n_heads):
            parts += [vt[hd * dv:(hd + 1) * dv], ones_row]
        return jnp.concatenate(parts, axis=0)

    qta_ref[0] = (tr[0:256] * (SB_DIM ** -0.5 * LOG2E)).astype(BF16)
    put_rows(ka_ref, nat[:, 0:256])
    put_cols(vta_ref, tr[256:512])

    qb = tr[512:1024]
    parts = []
    for ch in range(2 * DIFF_HEADS):
        base = ch * DIFF_QK
        r1, r2 = _rope_rows(qb, base, DIFF_ROT_HALF, cos_d, sin_d)
        parts += [r1, r2, qb[base + 2 * DIFF_ROT_HALF:base + DIFF_QK]]
    qtb_ref[0] = (jnp.concatenate(parts, axis=0) * (DIFF_QK ** -0.5 * LOG2E)).astype(BF16)
    kb = nat[:, 256:768]
    c_d, sp_d, sm_d = tabn[:, 0:128], tabn[:, 128:256], tabn[:, 256:384]
    kb = jnp.concatenate(
        [_rope_lanes(kb[:, c * LANES:(c + 1) * LANES], c_d, sp_d, sm_d, DIFF_ROT_HALF) for c in range(4)], axis=1)
    put_rows(kb_ref, kb)
    put_cols(vtb_ref, with_ones(tr[1024:1536], DIFF_HEADS, DIFF_V))

    cq = _rms_norm(nat[:, 768:1024], qn_ref[...]).astype(BF16)
    qc = _dot_nt(wuq_ref[...], cq)
    parts = []
    for hd in range(MLA_HEADS):
        base = hd * LANES
        r1, r2 = _rope_rows(qc, base + MLA_NOPE, MLA_ROT_HALF, cos_m, sin_m)
        parts += [qc[base:base + MLA_NOPE], r1, r2, qc[base + MLA_NOPE + MLA_ROPE:base + LANES]]
    qtc_ref[0] = (jnp.concatenate(parts, axis=0) * ((MLA_NOPE + MLA_ROPE) ** -0.5 * LOG2E)).astype(BF16)
    ckv = _rms_norm(nat[:, 1024:1152], kvn_ref[...]).astype(BF16)
    k_nope = _dot(ckv, wuk_ref[...])
    c_m, sp_m, sm_m = tabn[:, 384:512], tabn[:, 512:640], tabn[:, 640:768]
    k_rot = _rope_lanes(nat[:, 1152:1280], c_m, sp_m, sm_m, MLA_ROT_HALF)
    kc = jnp.concatenate([k_nope[:, hd * LANES:(hd + 1) * LANES] + k_rot for hd in range(MLA_HEADS)], axis=1)
    put_rows(kc_ref, kc)
    put_cols(vtc_ref, with_ones(_dot_nt(wuv_ref[...], ckv), MLA_HEADS, MLA_V))


def _proj(h, gain, wnat, wtr, qn, kvn, wuq, wuk, wuv, tabn, tabt, layer, batch, seq):
    t = h.shape[0]
    nblk = seq // ROW_TILE
    n_sub = ROW_TILE // ATT_TILE
    nkv = seq // ATT_TILE
    grid = (batch, nblk)

    def row_blocked(n):
        return (jax.ShapeDtypeStruct((batch, nkv, ATT_TILE, n), BF16),
                pl.BlockSpec((1, n_sub, ATT_TILE, n), lambda b, i: (b, i, 0, 0)))

    def col_blocked(n):
        return (jax.ShapeDtypeStruct((batch, nkv, n, ATT_TILE), BF16),
                pl.BlockSpec((1, n_sub, n, ATT_TILE), lambda b, i: (b, i, 0, 0)))

    def q_t(n):
        return (jax.ShapeDtypeStruct((batch, n, seq), BF16),
                pl.BlockSpec((1, n, ROW_TILE), lambda b, i: (b, 0, i)))

    outs = [q_t(256), row_blocked(256), col_blocked(256),
            q_t(512), row_blocked(512), col_blocked(DIFF_HEADS * DIFF_VT),
            q_t(512), row_blocked(512), col_blocked(MLA_HEADS * MLA_VT)]
    whole = lambda shape: pl.BlockSpec((None,) + shape, lambda b, i: (layer,) + (0,) * len(shape))
    return pl.pallas_call(
        _proj_kernel,
        grid=grid,
        in_specs=[
            pl.BlockSpec((ROW_TILE, D_MODEL), lambda b, i: (b * nblk + i, 0)),
            whole((1, D_MODEL)),
            whole((D_MODEL, N_NAT)),
            whole((N_TR, D_MODEL)),
            whole((1, MLA_Q_RANK)),
            whole((1, MLA_KV_RANK)),
            whole((4 * LANES, MLA_Q_RANK)),
            whole((MLA_KV_RANK, 4 * LANES)),
            whole((MLA_HEADS * MLA_V, MLA_KV_RANK)),
            pl.BlockSpec((ROW_TILE, 6 * LANES), lambda b, i: (b * nblk + i, 0)),
            pl.BlockSpec((1, 48, ROW_TILE), lambda b, i: (b, 0, i)),
        ],
        out_specs=[o[1] for o in outs],
        out_shape=[o[0] for o in outs],
        compiler_params=_params(("parallel", "parallel")),
        name="proj",
    )(h, gain, wnat, wtr, qn, kvn, wuq, wuk, wuv, tabn, tabt)


def _row_mask(n_rows, lo, hi, n_cols):
    r = lax.broadcasted_iota(jnp.int32, (n_rows, n_cols), 0)
    return jnp.where((r >= lo) & (r < hi), 1.0, 0.0).astype(BF16)


def _causal_softmax_sweep(i, q_heads, k_ref, vt_ref, m_ref, acc_ref, vt_rows):
    ta = ATT_TILE
    n = q_heads[0].shape[1]
    kk = lax.broadcasted_iota(jnp.int32, (ta, n), 0)
    qq = lax.broadcasted_iota(jnp.int32, (ta, n), 1) & (ta - 1)
    visible = kk <= qq

    def block(j, diag):
        k_blk = k_ref[0, j]
        vt_blk = vt_ref[0, j]
        n_heads = len(q_heads)
        st = [dict() for _ in range(n_heads)]

        def scores(hd):
            s = _dot(k_blk[:, hd * LANES:(hd + 1) * LANES], q_heads[hd])
            st[hd]["s"] = jnp.where(visible, s, MASK_VALUE) if diag else s

        def probs(hd):
            s = st[hd].pop("s")
            m_old = m_ref[hd]
            m_new = jnp.maximum(m_old, jnp.max(s, axis=0, keepdims=True))
            st[hd]["alpha"] = jnp.exp2(m_old - m_new)
            st[hd]["p"] = jnp.exp2(s - m_new).astype(BF16)
            m_ref[hd] = m_new

        def accumulate(hd):
            pv = _dot(vt_blk[hd * vt_rows:(hd + 1) * vt_rows], st[hd].pop("p"))
            acc_ref[hd] = st[hd].pop("alpha") * acc_ref[hd] + pv

        stages = (scores, probs, accumulate)
        for step in range(n_heads + len(stages) - 1):
            for s_idx in reversed(range(len(stages))):
                hd = step - s_idx
                if 0 <= hd < n_heads:
                    stages[s_idx](hd)

    def body(j, carry):
        block(j, False)
        return carry

    lax.fori_loop(0, i, body, 0)
    block(i, True)


def _diff_attn_kernel(lq1_ref, lk1_ref, lq2_ref, lk2_ref, subln_ref, qt_ref, k_ref, vt_ref, o_ref,
                      m_ref, acc_ref, *, lambda_init):
    i = pl.program_id(1)
    ta = ATT_TILE
    m_ref[...] = jnp.full_like(m_ref, MASK_VALUE)
    acc_ref[...] = jnp.zeros_like(acc_ref)

    qt = qt_ref[0]
    q_heads = []
    for hd in range(DIFF_HEADS):
        qh = qt[hd * LANES:(hd + 1) * LANES]
        q_heads.append(jnp.concatenate(
            [qh * _row_mask(LANES, 0, DIFF_QK, ta), qh * _row_mask(LANES, DIFF_QK, LANES, ta)], axis=1))

    _causal_softmax_sweep(i, q_heads, k_ref, vt_ref, m_ref, acc_ref, DIFF_VT)

    lam = (jnp.exp(jnp.sum(lq1_ref[...] * lk1_ref[...], axis=-1, keepdims=True))
           - jnp.exp(jnp.sum(lq2_ref[...] * lk2_ref[...], axis=-1, keepdims=True)) + lambda_init)
    outs = []
    for hd in range(DIFF_HEADS):
        acc = acc_ref[hd]
        o12 = acc[0:DIFF_V] / acc[DIFF_V:DIFF_V + 1]
        o = o12[:, 0:ta] - lam * o12[:, ta:2 * ta]
        ms = jnp.mean(o * o, axis=0, keepdims=True)
        outs.append(o * lax.rsqrt(ms + EPS))
    o_nat = jnp.concatenate(outs, axis=0).T
    o_ref[0] = (o_nat * subln_ref[...] * (1.0 - lambda_init)).astype(o_ref.dtype)


def _diff_attn(lq1, lk1, lq2, lk2, subln4, qt, k, vt, layer, batch, seq):
    nq = seq // ATT_TILE
    lambda_init = 0.8 - 0.6 * math.exp(-0.3 * layer)
    vec = lambda n: pl.BlockSpec((None, 1, n), lambda b, i: (layer, 0, 0))
    return pl.pallas_call(
        functools.partial(_diff_attn_kernel, lambda_init=lambda_init),
        grid=(batch, nq),
        in_specs=[
            vec(DIFF_QK), vec(DIFF_QK), vec(DIFF_QK), vec(DIFF_QK), vec(DIFF_HEADS * DIFF_V),
            pl.BlockSpec((1, 512, ATT_TILE), lambda b, i: (b, 0, i)),
            pl.BlockSpec((1, nq, ATT_TILE, 512), lambda b, i: (b, 0, 0, 0)),
            pl.BlockSpec((1, nq, DIFF_HEADS * DIFF_VT, ATT_TILE), lambda b, i: (b, 0, 0, 0)),
        ],
        out_specs=pl.BlockSpec((1, ATT_TILE, 512), lambda b, i: (b, i, 0)),
        out_shape=jax.ShapeDtypeStruct((batch, seq, 512), BF16),
        scratch_shapes=[pltpu.VMEM((DIFF_HEADS, 1, 2 * ATT_TILE), F32),
                        pltpu.VMEM((DIFF_HEADS, DIFF_VT, 2 * ATT_TILE), F32)],
        compiler_params=_params(("parallel", "arbitrary")),
        name="diff_attn",
    )(lq1, lk1, lq2, lk2, subln4, qt, k, vt)


def _mla_attn_kernel(qt_ref, k_ref, vt_ref, o_ref, m_ref, acc_ref):
    i = pl.program_id(1)
    m_ref[...] = jnp.full_like(m_ref, MASK_VALUE)
    acc_ref[...] = jnp.zeros_like(acc_ref)
    qt = qt_ref[0]
    q_heads = [qt[hd * LANES:(hd + 1) * LANES] for hd in range(MLA_HEADS)]
    _causal_softmax_sweep(i, q_heads, k_ref, vt_ref, m_ref, acc_ref, MLA_VT)
    outs = []
    for hd in range(MLA_HEADS):
        acc = acc_ref[hd]
        outs.append(acc[0:MLA_V] / acc[MLA_V:MLA_V + 1])
    o_ref[0] = jnp.concatenate(outs, axis=0).T.astype(o_ref.dtype)


def _mla_attn(qt, k, vt, batch, seq):
    nq = seq // ATT_TILE
    return pl.pallas_call(
        _mla_attn_kernel,
        grid=(batch, nq),
        in_specs=[
            pl.BlockSpec((1, 512, ATT_TILE), lambda b, i: (b, 0, i)),
            pl.BlockSpec((1, nq, ATT_TILE, 512), lambda b, i: (b, 0, 0, 0)),
            pl.BlockSpec((1, nq, MLA_HEADS * MLA_VT, ATT_TILE), lambda b, i: (b, 0, 0, 0)),
        ],
        out_specs=pl.BlockSpec((1, ATT_TILE, 256), lambda b, i: (b, i, 0)),
        out_shape=jax.ShapeDtypeStruct((batch, seq, 256), BF16),
        scratch_shapes=[pltpu.VMEM((MLA_HEADS, 1, ATT_TILE), F32),
                        pltpu.VMEM((MLA_HEADS, MLA_VT, ATT_TILE), F32)],
        compiler_params=_params(("parallel", "arbitrary")),
        name="mla_attn",
    )(qt, k, vt)


def _sb_attn_kernel(qt_ref, k_ref, vt_ref, o_ref, c_ref, acc_ref):
    i = pl.program_id(1)
    ta = ATT_TILE
    c_ref[...] = jnp.zeros_like(c_ref)
    acc_ref[...] = jnp.zeros_like(acc_ref)
    qt = qt_ref[0]
    q_masked = []
    for hd in range(SB_HEADS):
        qh = qt[(hd // 2) * LANES:(hd // 2 + 1) * LANES]
        lo = (hd % 2) * SB_DIM
        q_masked.append(qh * _row_mask(LANES, lo, lo + SB_DIM, ta))
    valid = (lax.broadcasted_iota(jnp.int32, (ta, ta), 0)
             < lax.broadcasted_iota(jnp.int32, (ta, ta), 1))
    half = MXU_DIM
    upper = jnp.where(lax.broadcasted_iota(jnp.int32, (half, half), 1)
                      > lax.broadcasted_iota(jnp.int32, (half, half), 0), 1.0, 0.0).astype(BF16)

    sign_bit = jnp.uint32(0x80000000)

    def block(j, diag):
        k_blk = k_ref[0, j]
        vt_blk = vt_ref[0, j]
        st = [dict() for _ in range(SB_HEADS)]

        def logits(hd):
            kh = k_blk[:, (hd // 2) * LANES:(hd // 2 + 1) * LANES]
            st[hd]["y"] = _dot(kh, q_masked[hd])

        def log_terms(hd):
            y = st[hd].pop("y")
            neg_abs = pltpu.bitcast(pltpu.bitcast(y, jnp.uint32) | sign_bit, F32)
            sp = jnp.log2(1.0 + jnp.exp2(neg_abs))
            log_beta = jnp.minimum(y, 0.0) - sp
            log_keep = log_beta - y
            if diag:
                log_keep = jnp.where(valid, log_keep, 0.0)
            st[hd].update(log_beta=log_beta, keep=log_keep.astype(BF16),
                          first=log_keep[0:1], mid=log_keep[half:half + 1])

        def suffix_sums(hd):
            keep = st[hd].pop("keep")
            st[hd]["later"] = _dot(upper, keep[half:])
            st[hd]["earlier"] = _dot(upper, keep[:half])

        def weights(hd):
            later, earlier = st[hd].pop("later"), st[hd].pop("earlier")
            c_old = c_ref[hd]
            later_total = later[0:1] + st[hd].pop("mid") + c_old
            between = jnp.concatenate([earlier + later_total, later + c_old], axis=0)
            a = jnp.exp2(st[hd].pop("log_beta") + between)
            if diag:
                a = jnp.where(valid, a, 0.0)
            c_ref[hd] = (earlier[0:1] + st[hd].pop("first")) + later_total
            st[hd]["a"] = a.astype(BF16)

        def accumulate(hd):
            acc_ref[hd] += _dot(vt_blk[hd * SB_DIM:(hd + 1) * SB_DIM], st[hd].pop("a"))

        stages = (logits, log_terms, suffix_sums, weights, accumulate)
        for step in range(SB_HEADS + len(stages) - 1):
            for s_idx in reversed(range(len(stages))):
                hd = step - s_idx
                if 0 <= hd < SB_HEADS:
                    stages[s_idx](hd)

    block(i, True)

    def body(jj, carry):
        block(i - 1 - jj, False)
        return carry

    lax.fori_loop(0, i, body, 0)
    o_ref[0] = acc_ref[...].reshape(SB_HEADS * SB_DIM, ta).T.astype(o_ref.dtype)


def _sb_attn(qt, k, vt, batch, seq):
    nq = seq // ATT_TILE
    return pl.pallas_call(
        _sb_attn_kernel,
        grid=(batch, nq),
        in_specs=[
            pl.BlockSpec((1, 256, ATT_TILE), lambda b, i: (b, 0, i)),
            pl.BlockSpec((1, nq, ATT_TILE, 256), lambda b, i: (b, 0, 0, 0)),
            pl.BlockSpec((1, nq, 256, ATT_TILE), lambda b, i: (b, 0, 0, 0)),
        ],
        out_specs=pl.BlockSpec((1, ATT_TILE, 256), lambda b, i: (b, i, 0)),
        out_shape=jax.ShapeDtypeStruct((batch, seq, 256), BF16),
        scratch_shapes=[pltpu.VMEM((4, 1, ATT_TILE), F32), pltpu.VMEM((4, SB_DIM, ATT_TILE), F32)],
        compiler_params=_params(("parallel", "arbitrary")),
        name="sb_attn",
    )(qt, k, vt)


def _out_proj_kernel(h_ref, a_ref, b_ref, c_ref, wa_ref, wb_ref, wc_ref, o_ref):
    o_ref[...] = (h_ref[...] + _dot(a_ref[...], wa_ref[...]) + _dot(b_ref[...], wb_ref[...])
                  + _dot(c_ref[...], wc_ref[...]))


def _out_proj(h, oa, ob, oc, w_out, layer):
    t = h.shape[0]
    rows = lambda n: pl.BlockSpec((ROW_TILE, n), lambda i: (i, 0))
    return pl.pallas_call(
        _out_proj_kernel,
        grid=(t // ROW_TILE,),
        in_specs=[
            rows(D_MODEL), rows(256), rows(512), rows(256),
            pl.BlockSpec((None, 256, D_MODEL), lambda i: (layer, 0, 0)),
            pl.BlockSpec((None, 512, D_MODEL), lambda i: (layer, 0, 0)),
            pl.BlockSpec((None, 256, D_MODEL), lambda i: (layer, 0, 0)),
        ],
        out_specs=rows(D_MODEL),
        out_shape=jax.ShapeDtypeStruct((t, D_MODEL), F32),
        compiler_params=_params(("parallel",)),
        name="out_proj",
    )(h, oa, ob, oc, w_out[0], w_out[1], w_out[2])


def _ple_kernel(h_ref, p_ref, gain_ref, wg_ref, wp_ref, fin_ref, o_ref, *, final):
    h = h_ref[...]
    gate = jax.nn.sigmoid(_dot(_rms_norm(h, gain_ref[...]).astype(BF16), wg_ref[...]))
    out = h + _dot(p_ref[0].astype(BF16), wp_ref[...]) * gate
    if final:
        out = _rms_norm(out, fin_ref[...])
    o_ref[...] = out


def _ple(h, p, gain, w_gate, w_proj, norm_final, layer, final):
    t = h.shape[0]
    return pl.pallas_call(
        functools.partial(_ple_kernel, final=final),
        grid=(t // ROW_TILE,),
        in_specs=[
            pl.BlockSpec((ROW_TILE, D_MODEL), lambda i: (i, 0)),
            pl.BlockSpec((1, ROW_TILE, PLE_DIM), lambda i: (layer, i, 0)),
            pl.BlockSpec((None, 1, D_MODEL), lambda i: (layer, 0, 0)),
            pl.BlockSpec((None, D_MODEL, D_MODEL), lambda i: (layer, 0, 0)),
            pl.BlockSpec((None, PLE_DIM, D_MODEL), lambda i: (layer, 0, 0)),
            pl.BlockSpec((1, D_MODEL), lambda i: (0, 0)),
        ],
        out_specs=pl.BlockSpec((ROW_TILE, D_MODEL), lambda i: (i, 0)),
        out_shape=jax.ShapeDtypeStruct((t, D_MODEL), F32),
        compiler_params=_params(("parallel",)),
        name="ple",
    )(h, p, gain, w_gate, w_proj, norm_final)


def _rope_tables(positions):
    b, s = positions.shape
    pos = positions.astype(F32)[..., None]

    def cs(rot_dim):
        inv_freq = 1.0 / (ROPE_THETA ** (jnp.arange(0, rot_dim, 2, dtype=F32) / rot_dim))
        ang = pos * inv_freq
        return jnp.cos(ang), jnp.sin(ang)

    cos_d, sin_d = cs(2 * DIFF_ROT_HALF)
    cos_m, sin_m = cs(2 * MLA_ROT_HALF)
    one = lambda n: jnp.ones((b, s, n), F32)
    zero = lambda n: jnp.zeros((b, s, n), F32)
    c_d = jnp.tile(jnp.concatenate([cos_d, cos_d, one(48)], -1), (1, 1, 2))
    sp_d = jnp.tile(jnp.concatenate([zero(8), sin_d, zero(48)], -1), (1, 1, 2))
    sm_d = jnp.tile(jnp.concatenate([-sin_d, zero(56)], -1), (1, 1, 2))
    c_m = jnp.concatenate([one(64), cos_m, cos_m, one(32)], -1)
    sp_m = jnp.concatenate([zero(80), sin_m, zero(32)], -1)
    sm_m = jnp.concatenate([zero(64), -sin_m, zero(48)], -1)
    tabn = jnp.concatenate([c_d, sp_d, sm_d, c_m, sp_m, sm_m], -1).reshape(b * s, 6 * LANES)
    tabt = jnp.swapaxes(jnp.concatenate([cos_d, sin_d, cos_m, sin_m], -1), 1, 2)
    return tabn, tabt


def _prep_mixer_weights(w_in, mla_w_uq, mla_w_ukv):
    n_l = w_in.shape[0]
    seg = lambda i: w_in[:, :, _OFFS[i]:_OFFS[i + 1]]
    a_q, a_k, a_v, b_q, b_k, b_v, c_q, c_kv, k_pe = (seg(i) for i in range(9))
    zeros = lambda n: jnp.zeros((n_l, D_MODEL, n), w_in.dtype)
    k_pe_pad = jnp.concatenate([zeros(MLA_NOPE), k_pe, zeros(LANES - MLA_NOPE - MLA_ROPE)], -1)
    wnat = jnp.concatenate([a_k, b_k, c_q, c_kv, k_pe_pad], -1).astype(BF16)
    wtr = jnp.swapaxes(jnp.concatenate([a_q, a_v, b_q, b_v], -1), 1, 2).astype(BF16)
    uq = mla_w_uq.reshape(n_l, MLA_Q_RANK, MLA_HEADS, MLA_NOPE + MLA_ROPE)
    uq = jnp.pad(uq, ((0, 0), (0, 0), (0, 0), (0, LANES - MLA_NOPE - MLA_ROPE)))
    wuq = jnp.swapaxes(uq.reshape(n_l, MLA_Q_RANK, MLA_HEADS * LANES), 1, 2).astype(BF16)
    ukv = mla_w_ukv.reshape(n_l, MLA_KV_RANK, MLA_HEADS, MLA_NOPE + MLA_V)
    uk = jnp.pad(ukv[..., :MLA_NOPE], ((0, 0), (0, 0), (0, 0), (0, LANES - MLA_NOPE)))
    wuk = uk.reshape(n_l, MLA_KV_RANK, MLA_HEADS * LANES).astype(BF16)
    wuv = jnp.swapaxes(ukv[..., MLA_NOPE:].reshape(n_l, MLA_KV_RANK, MLA_HEADS * MLA_V), 1, 2).astype(BF16)
    return wnat, wtr, wuq, wuk, wuv


def kernel(x, p, positions, norm_ffn1, w_ffn1_gu, w_ffn1_down, norm_mix, w_in, mla_q_norm, mla_w_uq,
           mla_kv_norm, mla_w_ukv, diff_lambda_q1, diff_lambda_k1, diff_lambda_q2, diff_lambda_k2,
           diff_subln, w_out, norm_ffn2, w_ffn2_gu, w_ffn2_down, norm_ple, w_ple_gate, w_ple_proj,
           norm_final):
    batch, seq, _ = x.shape
    depth = w_in.shape[0]
    assert seq % ROW_TILE == 0 and ROW_TILE % ATT_TILE == 0
    t = batch * seq
    row3 = lambda a: a.reshape(a.shape[0], 1, a.shape[1])

    tabn, tabt = _rope_tables(positions)
    wnat, wtr, wuq, wuk, wuv = _prep_mixer_weights(w_in, mla_w_uq, mla_w_ukv)
    w1_gu, w1_d = w_ffn1_gu.astype(BF16), w_ffn1_down.astype(BF16)
    w2_gu, w2_d = w_ffn2_gu.astype(BF16), w_ffn2_down.astype(BF16)
    w_out_b = w_out.astype(BF16)
    w_out_parts = (w_out_b[:, 0:256], w_out_b[:, 256:768], w_out_b[:, 768:1024])
    w_gate_b, w_pproj_b = w_ple_gate.astype(BF16), w_ple_proj.astype(BF16)
    g_ffn1, g_mix, g_ffn2, g_ple = row3(norm_ffn1), row3(norm_mix), row3(norm_ffn2), row3(norm_ple)
    g_q, g_kv = row3(mla_q_norm), row3(mla_kv_norm)
    lq1, lk1, lq2, lk2 = (row3(a) for a in (diff_lambda_q1, diff_lambda_k1, diff_lambda_q2, diff_lambda_k2))
    subln4 = row3(jnp.tile(diff_subln, (1, DIFF_HEADS)))
    p_rows = p.reshape(depth, t, PLE_DIM)
    fin = norm_final.reshape(1, D_MODEL)

    h = x.reshape(t, D_MODEL)
    for layer in range(depth):
        h = _ffn(h, g_ffn1, w1_gu, w1_d, layer)
        qta, ka, vta, qtb, kb, vtb, qtc, kc, vtc = _proj(
            h, g_mix, wnat, wtr, g_q, g_kv, wuq, wuk, wuv, tabn, tabt, layer, batch, seq)
        oa = _sb_attn(qta, ka, vta, batch, seq)
        ob = _diff_attn(lq1, lk1, lq2, lk2, subln4, qtb, kb, vtb, layer, batch, seq)
        oc = _mla_attn(qtc, kc, vtc, batch, seq)
        h = _out_proj(h, oa.reshape(t, 256), ob.reshape(t, 512), oc.reshape(t, 256), w_out_parts, layer)
        h = _ffn(h, g_ffn2, w2_gu, w2_d, layer)
        h = _ple(h, p_rows, g_ple, w_gate_b, w_pproj_b, fin, layer, final=(layer == depth - 1))
    return h.reshape(batch, seq, D_MODEL)
```

```python
import functools
import math

import jax
import jax.numpy as jnp
from jax import lax
from jax.experimental import pallas as pl
from jax.experimental.pallas import tpu as pltpu

F32 = jnp.float32
BF16 = jnp.bfloat16

D_MODEL = 1024
SB_HEADS, SB_DIM = 4, 64
DIFF_HEADS, DIFF_QK, DIFF_V = 4, 64, 128
MLA_HEADS, MLA_NOPE, MLA_ROPE, MLA_V = 4, 64, 32, 64
MLA_Q_RANK, MLA_KV_RANK = 256, 128
D_FF = 2816
PLE_DIM = 256
ROPE_THETA = 500000.0
DIFF_ROT_HALF = 8
MLA_ROT_HALF = 16
EPS = 1e-6
LOG2E = 1.4426950408889634
MASK_VALUE = -1e30

LANES = 128
MXU_DIM = 256
VMEM_LIMIT_BYTES = 56 * 1024 * 1024

ROW_TILE = 512
FF_CHUNK = 2 * MXU_DIM
ATT_TILE = 2 * MXU_DIM
V_PAD = 16
DIFF_VT = DIFF_V + V_PAD
MLA_VT = MLA_V + V_PAD

_SPLITS = (256, 256, 256, 512, 512, 512, MLA_Q_RANK, MLA_KV_RANK, MLA_ROPE)
_OFFS = tuple(int(sum(_SPLITS[:i])) for i in range(len(_SPLITS) + 1))
N_NAT = 256 + 512 + MLA_Q_RANK + MLA_KV_RANK + LANES
N_TR = 256 + 256 + 512 + 512


def _params(sem):
    return pltpu.CompilerParams(dimension_semantics=sem, vmem_limit_bytes=VMEM_LIMIT_BYTES)


def _rms_norm(x, gain):
    ms = jnp.mean(x * x, axis=-1, keepdims=True)
    return x * lax.rsqrt(ms + EPS) * gain


def _dot(a, b):
    return jnp.dot(a, b, preferred_element_type=F32)


def _dot_nt(a, b):
    return lax.dot_general(a, b, (((1,), (1,)), ((), ())), preferred_element_type=F32)


def _ffn_kernel(h_ref, gain_ref, wg_ref, wv_ref, wd_ref, o_ref):
    h = h_ref[...]
    xn = _rms_norm(h, gain_ref[...]).astype(BF16)
    acc = None
    for lo in range(0, D_FF, FF_CHUNK):
        hi = min(lo + FF_CHUNK, D_FF)
        g = _dot(xn, wg_ref[:, lo:hi])
        v = _dot(xn, wv_ref[:, lo:hi])
        a = (g * jax.nn.sigmoid(g) * v).astype(BF16)
        part = _dot(a, wd_ref[lo:hi, :])
        acc = part if acc is None else acc + part
    o_ref[...] = h + 0.5 * acc


def _ffn(h, gain, w_gu, w_down, layer):
    t = h.shape[0]
    resident = pl.Buffered(1)
    return pl.pallas_call(
        _ffn_kernel,
        grid=(t // ROW_TILE,),
        in_specs=[
            pl.BlockSpec((ROW_TILE, D_MODEL), lambda i: (i, 0)),
            pl.BlockSpec((None, 1, D_MODEL), lambda i: (layer, 0, 0)),
            pl.BlockSpec((None, D_MODEL, D_FF), lambda i: (layer, 0, 0), pipeline_mode=resident),
            pl.BlockSpec((None, D_MODEL, D_FF), lambda i: (layer, 0, 1), pipeline_mode=resident),
            pl.BlockSpec((None, D_FF, D_MODEL), lambda i: (layer, 0, 0), pipeline_mode=resident),
        ],
        out_specs=pl.BlockSpec((ROW_TILE, D_MODEL), lambda i: (i, 0)),
        out_shape=jax.ShapeDtypeStruct((t, D_MODEL), F32),
        compiler_params=_params(("parallel",)),
        name="ffn",
    )(h, gain, w_gu, w_gu, w_down)


def _rope_lanes(x, c, sp, sm, shift):
    return x * c + pltpu.roll(x, shift, 1) * sp + pltpu.roll(x, LANES - shift, 1) * sm


def _rope_rows(x, base, half, cos, sin):
    x1 = x[base:base + half]
    x2 = x[base + half:base + 2 * half]
    return x1 * cos - x2 * sin, x2 * cos + x1 * sin


def _proj_kernel(h_ref, gain_ref, wnat_ref, wtr_ref, qn_ref, kvn_ref, wuq_ref, wuk_ref, wuv_ref,
                 tabn_ref, tabt_ref,
                 qta_ref, ka_ref, vta_ref, qtb_ref, kb_ref, vtb_ref, qtc_ref, kc_ref, vtc_ref):
    n_sub = ROW_TILE // ATT_TILE
    u = _rms_norm(h_ref[...], gain_ref[...]).astype(BF16)
    nat = _dot(u, wnat_ref[...])
    tr = _dot_nt(wtr_ref[...], u)

    tabn = tabn_ref[...]
    tabt = tabt_ref[0]
    cos_d, sin_d = tabt[0:8], tabt[8:16]
    cos_m, sin_m = tabt[16:32], tabt[32:48]

    def put_rows(ref, val):
        for c in range(n_sub):
            ref[0, c] = val[c * ATT_TILE:(c + 1) * ATT_TILE].astype(ref.dtype)

    def put_cols(ref, val):
        for c in range(n_sub):
            ref[0, c] = val[:, c * ATT_TILE:(c + 1) * ATT_TILE].astype(ref.dtype)

    tm = h_ref.shape[0]
    ones_row = jnp.where(lax.broadcasted_iota(jnp.int32, (V_PAD, tm), 0) == 0, 1.0, 0.0)

    def with_ones(vt, n_heads, dv):
        parts = []
        for hd in range(n_heads):
            parts += [vt[hd * dv:(hd + 1) * dv], ones_row]
        return jnp.concatenate(parts, axis=0)

    qta_ref[0] = (tr[0:256] * (SB_DIM ** -0.5 * LOG2E)).astype(BF16)
    put_rows(ka_ref, nat[:, 0:256])
    put_cols(vta_ref, tr[256:512])

    qb = tr[512:1024]
    parts = []
    for ch in range(2 * DIFF_HEADS):
        base = ch * DIFF_QK
        r1, r2 = _rope_rows(qb, base, DIFF_ROT_HALF, cos_d, sin_d)
        parts += [r1, r2, qb[base + 2 * DIFF_ROT_HALF:base + DIFF_QK]]
    qtb_ref[0] = (jnp.concatenate(parts, axis=0) * (DIFF_QK ** -0.5 * LOG2E)).astype(BF16)
    kb = nat[:, 256:768]
    c_d, sp_d, sm_d = tabn[:, 0:128], tabn[:, 128:256], tabn[:, 256:384]
    kb = jnp.concatenate(
        [_rope_lanes(kb[:, c * LANES:(c + 1) * LANES], c_d, sp_d, sm_d, DIFF_ROT_HALF) for c in range(4)], axis=1)
    put_rows(kb_ref, kb)
    put_cols(vtb_ref, with_ones(tr[1024:1536], DIFF_HEADS, DIFF_V))

    cq = _rms_norm(nat[:, 768:1024], qn_ref[...]).astype(BF16)
    qc = _dot_nt(wuq_ref[...], cq)
    parts = []
    for hd in range(MLA_HEADS):
        base = hd * LANES
        r1, r2 = _rope_rows(qc, base + MLA_NOPE, MLA_ROT_HALF, cos_m, sin_m)
        parts += [qc[base:base + MLA_NOPE], r1, r2, qc[base + MLA_NOPE + MLA_ROPE:base + LANES]]
    qtc_ref[0] = (jnp.concatenate(parts, axis=0) * ((MLA_NOPE + MLA_ROPE) ** -0.5 * LOG2E)).astype(BF16)
    ckv = _rms_norm(nat[:, 1024:1152], kvn_ref[...]).astype(BF16)
    k_nope = _dot(ckv, wuk_ref[...])
    c_m, sp_m, sm_m = tabn[:, 384:512], tabn[:, 512:640], tabn[:, 640:768]
    k_rot = _rope_lanes(nat[:, 1152:1280], c_m, sp_m, sm_m, MLA_ROT_HALF)
    kc = jnp.concatenate([k_nope[:, hd * LANES:(hd + 1) * LANES] + k_rot for hd in range(MLA_HEADS)], axis=1)
    put_rows(kc_ref, kc)
    put_cols(vtc_ref, with_ones(_dot_nt(wuv_ref[...], ckv), MLA_HEADS, MLA_V))


def _proj(h, gain, wnat, wtr, qn, kvn, wuq, wuk, wuv, tabn, tabt, layer, batch, seq):
    t = h.shape[0]
    nblk = seq // ROW_TILE
    n_sub = ROW_TILE // ATT_TILE
    nkv = seq // ATT_TILE
    grid = (batch, nblk)

    def row_blocked(n):
        return (jax.ShapeDtypeStruct((batch, nkv, ATT_TILE, n), BF16),
                pl.BlockSpec((1, n_sub, ATT_TILE, n), lambda b, i: (b, i, 0, 0)))

    def col_blocked(n):
        return (jax.ShapeDtypeStruct((batch, nkv, n, ATT_TILE), BF16),
                pl.BlockSpec((1, n_sub, n, ATT_TILE), lambda b, i: (b, i, 0, 0)))

    def q_t(n):
        return (jax.ShapeDtypeStruct((batch, n, seq), BF16),
                pl.BlockSpec((1, n, ROW_TILE), lambda b, i: (b, 0, i)))

    outs = [q_t(256), row_blocked(256), col_blocked(256),
            q_t(512), row_blocked(512), col_blocked(DIFF_HEADS * DIFF_VT),
            q_t(512), row_blocked(512), col_blocked(MLA_HEADS * MLA_VT)]
    whole = lambda shape: pl.BlockSpec((None,) + shape, lambda b, i: (layer,) + (0,) * len(shape))
    return pl.pallas_call(
        _proj_kernel,
        grid=grid,
        in_specs=[
            pl.BlockSpec((ROW_TILE, D_MODEL), lambda b, i: (b * nblk + i, 0)),
            whole((1, D_MODEL)),
            whole((D_MODEL, N_NAT)),
            whole((N_TR, D_MODEL)),
            whole((1, MLA_Q_RANK)),
            whole((1, MLA_KV_RANK)),
            whole((4 * LANES, MLA_Q_RANK)),
            whole((MLA_KV_RANK, 4 * LANES)),
            whole((MLA_HEADS * MLA_V, MLA_KV_RANK)),
            pl.BlockSpec((ROW_TILE, 6 * LANES), lambda b, i: (b * nblk + i, 0)),
            pl.BlockSpec((1, 48, ROW_TILE), lambda b, i: (b, 0, i)),
        ],
        out_specs=[o[1] for o in outs],
        out_shape=[o[0] for o in outs],
        compiler_params=_params(("parallel", "parallel")),
        name="proj",
    )(h, gain, wnat, wtr, qn, kvn, wuq, wuk, wuv, tabn, tabt)


def _row_mask(n_rows, lo, hi, n_cols):
    r = lax.broadcasted_iota(jnp.int32, (n_rows, n_cols), 0)
    return jnp.where((r >= lo) & (r < hi), 1.0, 0.0).astype(BF16)


def _causal_softmax_sweep(i, q_heads, k_ref, vt_ref, m_ref, acc_ref, vt_rows):
    ta = ATT_TILE
    n = q_heads[0].shape[1]
    kk = lax.broadcasted_iota(jnp.int32, (ta, n), 0)
    qq = lax.broadcasted_iota(jnp.int32, (ta, n), 1) & (ta - 1)
    visible = kk <= qq

    def block(j, diag):
        k_blk = k_ref[0, j]
        vt_blk = vt_ref[0, j]
        n_heads = len(q_heads)
        st = [dict() for _ in range(n_heads)]

        def scores(hd):
            s = _dot(k_blk[:, hd * LANES:(hd + 1) * LANES], q_heads[hd])
            st[hd]["s"] = jnp.where(visible, s, MASK_VALUE) if diag else s

        def probs(hd):
            s = st[hd].pop("s")
            m_old = m_ref[hd]
            m_new = jnp.maximum(m_old, jnp.max(s, axis=0, keepdims=True))
            st[hd]["alpha"] = jnp.exp2(m_old - m_new)
            st[hd]["p"] = jnp.exp2(s - m_new).astype(BF16)
            m_ref[hd] = m_new

        def accumulate(hd):
            pv = _dot(vt_blk[hd * vt_rows:(hd + 1) * vt_rows], st[hd].pop("p"))
            acc_ref[hd] = st[hd].pop("alpha") * acc_ref[hd] + pv

        stages = (scores, probs, accumulate)
        for step in range(n_heads + len(stages) - 1):
            for s_idx in reversed(range(len(stages))):
                hd = step - s_idx
                if 0 <= hd < n_heads:
                    stages[s_idx](hd)

    def body(j, carry):
        block(j, False)
        return carry

    lax.fori_loop(0, i, body, 0)
    block(i, True)


def _diff_attn_kernel(lq1_ref, lk1_ref, lq2_ref, lk2_ref, subln_ref, qt_ref, k_ref, vt_ref, o_ref,
                      m_ref, acc_ref, *, lambda_init):
    i = pl.program_id(1)
    ta = ATT_TILE
    m_ref[...] = jnp.full_like(m_ref, MASK_VALUE)
    acc_ref[...] = jnp.zeros_like(acc_ref)

    qt = qt_ref[0]
    q_heads = []
    for hd in range(DIFF_HEADS):
        qh = qt[hd * LANES:(hd + 1) * LANES]
        q_heads.append(jnp.concatenate(
            [qh * _row_mask(LANES, 0, DIFF_QK, ta), qh * _row_mask(LANES, DIFF_QK, LANES, ta)], axis=1))

    _causal_softmax_sweep(i, q_heads, k_ref, vt_ref, m_ref, acc_ref, DIFF_VT)

    lam = (jnp.exp(jnp.sum(lq1_ref[...] * lk1_ref[...], axis=-1, keepdims=True))
           - jnp.exp(jnp.sum(lq2_ref[...] * lk2_ref[...], axis=-1, keepdims=True)) + lambda_init)
    outs = []
    for hd in range(DIFF_HEADS):
        acc = acc_ref[hd]
        o12 = acc[0:DIFF_V] / acc[DIFF_V:DIFF_V + 1]
        o = o12[:, 0:ta] - lam * o12[:, ta:2 * ta]
        ms = jnp.mean(o * o, axis=0, keepdims=True)
        outs.append(o * lax.rsqrt(ms + EPS))
    o_nat = jnp.concatenate(outs, axis=0).T
    o_ref[0] = (o_nat * subln_ref[...] * (1.0 - lambda_init)).astype(o_ref.dtype)


def _diff_attn(lq1, lk1, lq2, lk2, subln4, qt, k, vt, layer, batch, seq):
    nq = seq // ATT_TILE
    lambda_init = 0.8 - 0.6 * math.exp(-0.3 * layer)
    vec = lambda n: pl.BlockSpec((None, 1, n), lambda b, i: (layer, 0, 0))
    return pl.pallas_call(
        functools.partial(_diff_attn_kernel, lambda_init=lambda_init),
        grid=(batch, nq),
        in_specs=[
            vec(DIFF_QK), vec(DIFF_QK), vec(DIFF_QK), vec(DIFF_QK), vec(DIFF_HEADS * DIFF_V),
            pl.BlockSpec((1, 512, ATT_TILE), lambda b, i: (b, 0, i)),
            pl.BlockSpec((1, nq, ATT_TILE, 512), lambda b, i: (b, 0, 0, 0)),
            pl.BlockSpec((1, nq, DIFF_HEADS * DIFF_VT, ATT_TILE), lambda b, i: (b, 0, 0, 0)),
        ],
        out_specs=pl.BlockSpec((1, ATT_TILE, 512), lambda b, i: (b, i, 0)),
        out_shape=jax.ShapeDtypeStruct((batch, seq, 512), BF16),
        scratch_shapes=[pltpu.VMEM((DIFF_HEADS, 1, 2 * ATT_TILE), F32),
                        pltpu.VMEM((DIFF_HEADS, DIFF_VT, 2 * ATT_TILE), F32)],
        compiler_params=_params(("parallel", "arbitrary")),
        name="diff_attn",
    )(lq1, lk1, lq2, lk2, subln4, qt, k, vt)


def _mla_attn_kernel(qt_ref, k_ref, vt_ref, o_ref, m_ref, acc_ref):
    i = pl.program_id(1)
    m_ref[...] = jnp.full_like(m_ref, MASK_VALUE)
    acc_ref[...] = jnp.zeros_like(acc_ref)
    qt = qt_ref[0]
    q_heads = [qt[hd * LANES:(hd + 1) * LANES] for hd in range(MLA_HEADS)]
    _causal_softmax_sweep(i, q_heads, k_ref, vt_ref, m_ref, acc_ref, MLA_VT)
    outs = []
    for hd in range(MLA_HEADS):
        acc = acc_ref[hd]
        outs.append(acc[0:MLA_V] / acc[MLA_V:MLA_V + 1])
    o_ref[0] = jnp.concatenate(outs, axis=0).T.astype(o_ref.dtype)


def _mla_attn(qt, k, vt, batch, seq):
    nq = seq // ATT_TILE
    return pl.pallas_call(
        _mla_attn_kernel,
        grid=(batch, nq),
        in_specs=[
            pl.BlockSpec((1, 512, ATT_TILE), lambda b, i: (b, 0, i)),
            pl.BlockSpec((1, nq, ATT_TILE, 512), lambda b, i: (b, 0, 0, 0)),
            pl.BlockSpec((1, nq, MLA_HEADS * MLA_VT, ATT_TILE), lambda b, i: (b, 0, 0, 0)),
        ],
        out_specs=pl.BlockSpec((1, ATT_TILE, 256), lambda b, i: (b, i, 0)),
        out_shape=jax.ShapeDtypeStruct((batch, seq, 256), BF16),
        scratch_shapes=[pltpu.VMEM((MLA_HEADS, 1, ATT_TILE), F32),
                        pltpu.VMEM((MLA_HEADS, MLA_VT, ATT_TILE), F32)],
        compiler_params=_params(("parallel", "arbitrary")),
        name="mla_attn",
    )(qt, k, vt)


def _sb_attn_kernel(qt_ref, k_ref, vt_ref, o_ref, c_ref, acc_ref):
    i = pl.program_id(1)
    ta = ATT_TILE
    c_ref[...] = jnp.zeros_like(c_ref)
    acc_ref[...] = jnp.zeros_like(acc_ref)
    qt = qt_ref[0]
    q_masked = []
    for hd in range(SB_HEADS):
        qh = qt[(hd // 2) * LANES:(hd // 2 + 1) * LANES]
        lo = (hd % 2) * SB_DIM
        q_masked.append(qh * _row_mask(LANES, lo, lo + SB_DIM, ta))
    valid = (lax.broadcasted_iota(jnp.int32, (ta, ta), 0)
             < lax.broadcasted_iota(jnp.int32, (ta, ta), 1))
    half = MXU_DIM
    upper = jnp.where(lax.broadcasted_iota(jnp.int32, (half, half), 1)
                      > lax.broadcasted_iota(jnp.int32, (half, half), 0), 1.0, 0.0).astype(BF16)

    sign_bit = jnp.uint32(0x80000000)

    def block(j, diag):
        k_blk = k_ref[0, j]
        vt_blk = vt_ref[0, j]
        st = [dict() for _ in range(SB_HEADS)]

        def logits(hd):
            kh = k_blk[:, (hd // 2) * LANES:(hd // 2 + 1) * LANES]
            st[hd]["y"] = _dot(kh, q_masked[hd])

        def log_terms(hd):
            y = st[hd].pop("y")
            neg_abs = pltpu.bitcast(pltpu.bitcast(y, jnp.uint32) | sign_bit, F32)
            sp = jnp.log2(1.0 + jnp.exp2(neg_abs))
            log_beta = jnp.minimum(y, 0.0) - sp
            log_keep = log_beta - y
            if diag:
                log_keep = jnp.where(valid, log_keep, 0.0)
            st[hd].update(log_beta=log_beta, keep=log_keep.astype(BF16),
                          first=log_keep[0:1], mid=log_keep[half:half + 1])

        def suffix_sums(hd):
            keep = st[hd].pop("keep")
            st[hd]["later"] = _dot(upper, keep[half:])
            st[hd]["earlier"] = _dot(upper, keep[:half])

        def weights(hd):
            later, earlier = st[hd].pop("later"), st[hd].pop("earlier")
            c_old = c_ref[hd]
            later_total = later[0:1] + st[hd].pop("mid") + c_old
            between = jnp.concatenate([earlier + later_total, later + c_old], axis=0)
            a = jnp.exp2(st[hd].pop("log_beta") + between)
            if diag:
                a = jnp.where(valid, a, 0.0)
            c_ref[hd] = (earlier[0:1] + st[hd].pop("first")) + later_total
            st[hd]["a"] = a.astype(BF16)

        def accumulate(hd):
            acc_ref[hd] += _dot(vt_blk[hd * SB_DIM:(hd + 1) * SB_DIM], st[hd].pop("a"))

        stages = (logits, log_terms, suffix_sums, weights, accumulate)
        for step in range(SB_HEADS + len(stages) - 1):
            for s_idx in reversed(range(len(stages))):
                hd = step - s_idx
                if 0 <= hd < SB_HEADS:
                    stages[s_idx](hd)

    block(i, True)

    def body(jj, carry):
        block(i - 1 - jj, False)
        return carry

    lax.fori_loop(0, i, body, 0)
    o_ref[0] = acc_ref[...].reshape(SB_HEADS * SB_DIM, ta).T.astype(o_ref.dtype)


def _sb_attn(qt, k, vt, batch, seq):
    nq = seq // ATT_TILE
    return pl.pallas_call(
        _sb_attn_kernel,
        grid=(batch, nq),
        in_specs=[
            pl.BlockSpec((1, 256, ATT_TILE), lambda b, i: (b, 0, i)),
            pl.BlockSpec((1, nq, ATT_TILE, 256), lambda b, i: (b, 0, 0, 0)),
            pl.BlockSpec((1, nq, 256, ATT_TILE), lambda b, i: (b, 0, 0, 0)),
        ],
        out_specs=pl.BlockSpec((1, ATT_TILE, 256), lambda b, i: (b, i, 0)),
        out_shape=jax.ShapeDtypeStruct((batch, seq, 256), BF16),
        scratch_shapes=[pltpu.VMEM((4, 1, ATT_TILE), F32), pltpu.VMEM((4, SB_DIM, ATT_TILE), F32)],
        compiler_params=_params(("parallel", "arbitrary")),
        name="sb_attn",
    )(qt, k, vt)


def _emit_skewed(items):
    n_steps = max(k + len(stages) for k, stages in enumerate(items))
    for step in range(n_steps):
        for k, stages in enumerate(items):
            s_idx = step - k
            if 0 <= s_idx < len(stages):
                stages[s_idx]()


def _softmax_items(k_blk, vt_blk, q_heads, m_ref, acc_ref, vt_rows, visible):
    items = []
    for hd in range(len(q_heads)):
        st = {}

        def scores(hd=hd, st=st):
            s = _dot(k_blk[:, hd * LANES:(hd + 1) * LANES], q_heads[hd])
            st["s"] = s if visible is None else jnp.where(visible, s, MASK_VALUE)

        def probs(hd=hd, st=st):
            s = st.pop("s")
            m_old = m_ref[hd]
            m_new = jnp.maximum(m_old, jnp.max(s, axis=0, keepdims=True))
            st["alpha"] = jnp.exp2(m_old - m_new)
            st["p"] = jnp.exp2(s - m_new).astype(BF16)
            m_ref[hd] = m_new

        def accumulate(hd=hd, st=st):
            pv = _dot(vt_blk[hd * vt_rows:(hd + 1) * vt_rows], st.pop("p"))
            acc_ref[hd] = st.pop("alpha") * acc_ref[hd] + pv

        items.append([scores, probs, accumulate])
    return items


def _stick_breaking_items(k_blk, vt_blk, q_heads, c_ref, acc_ref, upper, valid):
    half = MXU_DIM
    sign_bit = jnp.uint32(0x80000000)
    items = []
    for hd in range(len(q_heads)):
        st = {}

        def logits(hd=hd, st=st):
            st["y"] = _dot(k_blk[:, (hd // 2) * LANES:(hd // 2 + 1) * LANES], q_heads[hd])

        def log_terms(hd=hd, st=st):
            y = st.pop("y")
            neg_abs = pltpu.bitcast(pltpu.bitcast(y, jnp.uint32) | sign_bit, F32)
            sp = jnp.log2(1.0 + jnp.exp2(neg_abs))
            log_beta = jnp.minimum(y, 0.0) - sp
            log_keep = log_beta - y
            if valid is not None:
                log_keep = jnp.where(valid, log_keep, 0.0)
            st.update(log_beta=log_beta, keep=log_keep.astype(BF16),
                      first=log_keep[0:1], mid=log_keep[half:half + 1])

        def suffix_sums(hd=hd, st=st):
            keep = st.pop("keep")
            st["later"] = _dot(upper, keep[half:])
            st["earlier"] = _dot(upper, keep[:half])

        def weights(hd=hd, st=st):
            later, earlier = st.pop("later"), st.pop("earlier")
            c_old = c_ref[hd]
            later_total = later[0:1] + st.pop("mid") + c_old
            between = jnp.concatenate([earlier + later_total, later + c_old], axis=0)
            a = jnp.exp2(st.pop("log_beta") + between)
            if valid is not None:
                a = jnp.where(valid, a, 0.0)
            c_ref[hd] = (earlier[0:1] + st.pop("first")) + later_total
            st["a"] = a.astype(BF16)

        def accumulate(hd=hd, st=st):
            acc_ref[hd] += _dot(vt_blk[hd * SB_DIM:(hd + 1) * SB_DIM], st.pop("a"))

        items.append([logits, log_terms, suffix_sums, weights, accumulate])
    return items


def _mixer_attn_kernel(lq1_ref, lk1_ref, lq2_ref, lk2_ref, subln_ref,
                       qta_ref, ka_ref, vta_ref, qtb_ref, kb_ref, vtb_ref, qtc_ref, kc_ref, vtc_ref,
                       o_ref, c_ref, acc_a_ref, m_b_ref, acc_b_ref, m_c_ref, acc_c_ref, *, lambda_init):
    i = pl.program_id(1)
    ta = ATT_TILE
    c_ref[...] = jnp.zeros_like(c_ref)
    acc_a_ref[...] = jnp.zeros_like(acc_a_ref)
    m_b_ref[...] = jnp.full_like(m_b_ref, MASK_VALUE)
    acc_b_ref[...] = jnp.zeros_like(acc_b_ref)
    m_c_ref[...] = jnp.full_like(m_c_ref, MASK_VALUE)
    acc_c_ref[...] = jnp.zeros_like(acc_c_ref)

    qta, qtb, qtc = qta_ref[0], qtb_ref[0], qtc_ref[0]
    q_a, q_b, q_c = [], [], []
    for hd in range(SB_HEADS):
        lo = (hd % 2) * SB_DIM
        q_a.append(qta[(hd // 2) * LANES:(hd // 2 + 1) * LANES] * _row_mask(LANES, lo, lo + SB_DIM, ta))
    for hd in range(DIFF_HEADS):
        qh = qtb[hd * LANES:(hd + 1) * LANES]
        q_b.append(jnp.concatenate(
            [qh * _row_mask(LANES, 0, DIFF_QK, ta), qh * _row_mask(LANES, DIFF_QK, LANES, ta)], axis=1))
    for hd in range(MLA_HEADS):
        q_c.append(qtc[hd * LANES:(hd + 1) * LANES])

    kk2 = lax.broadcasted_iota(jnp.int32, (ta, 2 * ta), 0)
    qq2 = lax.broadcasted_iota(jnp.int32, (ta, 2 * ta), 1) & (ta - 1)
    visible_b = kk2 <= qq2
    kk = lax.broadcasted_iota(jnp.int32, (ta, ta), 0)
    qq = lax.broadcasted_iota(jnp.int32, (ta, ta), 1)
    visible_c = kk <= qq
    valid_a = kk < qq
    upper = jnp.where(lax.broadcasted_iota(jnp.int32, (MXU_DIM, MXU_DIM), 1)
                      > lax.broadcasted_iota(jnp.int32, (MXU_DIM, MXU_DIM), 0), 1.0, 0.0).astype(BF16)

    def block(j, diag):
        a_items = _stick_breaking_items(ka_ref[0, j], vta_ref[0, j], q_a, c_ref, acc_a_ref, upper,
                                        valid_a if diag else None)
        b_items = _softmax_items(kb_ref[0, j], vtb_ref[0, j], q_b, m_b_ref, acc_b_ref, DIFF_VT,
                                 visible_b if diag else None)
        c_items = _softmax_items(kc_ref[0, j], vtc_ref[0, j], q_c, m_c_ref, acc_c_ref, MLA_VT,
                                 visible_c if diag else None)
        items = []
        for hd in range(SB_HEADS):
            items += [a_items[hd], b_items[hd], c_items[hd]]
        _emit_skewed(items)

    block(i, True)

    def body(jj, carry):
        block(i - 1 - jj, False)
        return carry

    lax.fori_loop(0, i, body, 0)

    o_ref[0, :, 0:256] = acc_a_ref[...].reshape(SB_HEADS * SB_DIM, ta).T.astype(o_ref.dtype)

    lam = (jnp.exp(jnp.sum(lq1_ref[...] * lk1_ref[...], axis=-1, keepdims=True))
           - jnp.exp(jnp.sum(lq2_ref[...] * lk2_ref[...], axis=-1, keepdims=True)) + lambda_init)
    outs = []
    for hd in range(DIFF_HEADS):
        acc = acc_b_ref[hd]
        o12 = acc[0:DIFF_V] / acc[DIFF_V:DIFF_V + 1]
        o = o12[:, 0:ta] - lam * o12[:, ta:2 * ta]
        ms = jnp.mean(o * o, axis=0, keepdims=True)
        outs.append(o * lax.rsqrt(ms + EPS))
    o_nat = jnp.concatenate(outs, axis=0).T
    o_ref[0, :, 256:768] = (o_nat * subln_ref[...] * (1.0 - lambda_init)).astype(o_ref.dtype)

    outs = []
    for hd in range(MLA_HEADS):
        acc = acc_c_ref[hd]
        outs.append(acc[0:MLA_V] / acc[MLA_V:MLA_V + 1])
    o_ref[0, :, 768:1024] = jnp.concatenate(outs, axis=0).T.astype(o_ref.dtype)


def _mixer_attn(lq1, lk1, lq2, lk2, subln4, qta, ka, vta, qtb, kb, vtb, qtc, kc, vtc, layer, batch, seq):
    nq = seq // ATT_TILE
    lambda_init = 0.8 - 0.6 * math.exp(-0.3 * layer)
    vec = lambda n: pl.BlockSpec((None, 1, n), lambda b, i: (layer, 0, 0))
    q_t = lambda n: pl.BlockSpec((1, n, ATT_TILE), lambda b, i: (b, 0, i))
    per_batch = lambda r, c: pl.BlockSpec((1, nq, r, c), lambda b, i: (b, 0, 0, 0),
                                          pipeline_mode=pl.Buffered(1))
    return pl.pallas_call(
        functools.partial(_mixer_attn_kernel, lambda_init=lambda_init),
        grid=(batch, nq),
        in_specs=[
            vec(DIFF_QK), vec(DIFF_QK), vec(DIFF_QK), vec(DIFF_QK), vec(DIFF_HEADS * DIFF_V),
            q_t(256), per_batch(ATT_TILE, 256), per_batch(256, ATT_TILE),
            q_t(512), per_batch(ATT_TILE, 512), per_batch(DIFF_HEADS * DIFF_VT, ATT_TILE),
            q_t(512), per_batch(ATT_TILE, 512), per_batch(MLA_HEADS * MLA_VT, ATT_TILE),
        ],
        out_specs=pl.BlockSpec((1, ATT_TILE, D_MODEL), lambda b, i: (b, i, 0)),
        out_shape=jax.ShapeDtypeStruct((batch, seq, D_MODEL), BF16),
        scratch_shapes=[
            pltpu.VMEM((SB_HEADS, 1, ATT_TILE), F32), pltpu.VMEM((SB_HEADS, SB_DIM, ATT_TILE), F32),
            pltpu.VMEM((DIFF_HEADS, 1, 2 * ATT_TILE), F32), pltpu.VMEM((DIFF_HEADS, DIFF_VT, 2 * ATT_TILE), F32),
            pltpu.VMEM((MLA_HEADS, 1, ATT_TILE), F32), pltpu.VMEM((MLA_HEADS, MLA_VT, ATT_TILE), F32),
        ],
        compiler_params=_params(("parallel", "arbitrary")),
        name="mixer_attn",
    )(lq1, lk1, lq2, lk2, subln4, qta, ka, vta, qtb, kb, vtb, qtc, kc, vtc)


def _out_proj_kernel(h_ref, mix_ref, w_ref, o_ref):
    o_ref[...] = h_ref[...] + _dot(mix_ref[...], w_ref[...])


def _out_proj(h, mixed, w_out, layer):
    t = h.shape[0]
    rows = lambda n: pl.BlockSpec((ROW_TILE, n), lambda i: (i, 0))
    return pl.pallas_call(
        _out_proj_kernel,
        grid=(t // ROW_TILE,),
        in_specs=[rows(D_MODEL), rows(D_MODEL),
                  pl.BlockSpec((None, D_MODEL, D_MODEL), lambda i: (layer, 0, 0))],
        out_specs=rows(D_MODEL),
        out_shape=jax.ShapeDtypeStruct((t, D_MODEL), F32),
        compiler_params=_params(("parallel",)),
        name="out_proj",
    )(h, mixed, w_out)


def _ple_kernel(h_ref, p_ref, gain_ref, wg_ref, wp_ref, fin_ref, o_ref, *, final):
    h = h_ref[...]
    gate = jax.nn.sigmoid(_dot(_rms_norm(h, gain_ref[...]).astype(BF16), wg_ref[...]))
    out = h + _dot(p_ref[0].astype(BF16), wp_ref[...]) * gate
    if final:
        out = _rms_norm(out, fin_ref[...])
    o_ref[...] = out


def _ple(h, p, gain, w_gate, w_proj, norm_final, layer, final):
    t = h.shape[0]
    return pl.pallas_call(
        functools.partial(_ple_kernel, final=final),
        grid=(t // ROW_TILE,),
        in_specs=[
            pl.BlockSpec((ROW_TILE, D_MODEL), lambda i: (i, 0)),
            pl.BlockSpec((1, ROW_TILE, PLE_DIM), lambda i: (layer, i, 0)),
            pl.BlockSpec((None, 1, D_MODEL), lambda i: (layer, 0, 0)),
            pl.BlockSpec((None, D_MODEL, D_MODEL), lambda i: (layer, 0, 0)),
            pl.BlockSpec((None, PLE_DIM, D_MODEL), lambda i: (layer, 0, 0)),
            pl.BlockSpec((1, D_MODEL), lambda i: (0, 0)),
        ],
        out_specs=pl.BlockSpec((ROW_TILE, D_MODEL), lambda i: (i, 0)),
        out_shape=jax.ShapeDtypeStruct((t, D_MODEL), F32),
        compiler_params=_params(("parallel",)),
        name="ple",
    )(h, p, gain, w_gate, w_proj, norm_final)


def _rope_tables(positions):
    b, s = positions.shape
    pos = positions.astype(F32)[..., None]

    def cs(rot_dim):
        inv_freq = 1.0 / (ROPE_THETA ** (jnp.arange(0, rot_dim, 2, dtype=F32) / rot_dim))
        ang = pos * inv_freq
        return jnp.cos(ang), jnp.sin(ang)

    cos_d, sin_d = cs(2 * DIFF_ROT_HALF)
    cos_m, sin_m = cs(2 * MLA_ROT_HALF)
    one = lambda n: jnp.ones((b, s, n), F32)
    zero = lambda n: jnp.zeros((b, s, n), F32)
    c_d = jnp.tile(jnp.concatenate([cos_d, cos_d, one(48)], -1), (1, 1, 2))
    sp_d = jnp.tile(jnp.concatenate([zero(8), sin_d, zero(48)], -1), (1, 1, 2))
    sm_d = jnp.tile(jnp.concatenate([-sin_d, zero(56)], -1), (1, 1, 2))
    c_m = jnp.concatenate([one(64), cos_m, cos_m, one(32)], -1)
    sp_m = jnp.concatenate([zero(80), sin_m, zero(32)], -1)
    sm_m = jnp.concatenate([zero(64), -sin_m, zero(48)], -1)
    tabn = jnp.concatenate([c_d, sp_d, sm_d, c_m, sp_m, sm_m], -1).reshape(b * s, 6 * LANES)
    tabt = jnp.swapaxes(jnp.concatenate([cos_d, sin_d, cos_m, sin_m], -1), 1, 2)
    return tabn, tabt


def _prep_mixer_weights(w_in, mla_w_uq, mla_w_ukv):
    n_l = w_in.shape[0]
    seg = lambda i: w_in[:, :, _OFFS[i]:_OFFS[i + 1]]
    a_q, a_k, a_v, b_q, b_k, b_v, c_q, c_kv, k_pe = (seg(i) for i in range(9))
    zeros = lambda n: jnp.zeros((n_l, D_MODEL, n), w_in.dtype)
    k_pe_pad = jnp.concatenate([zeros(MLA_NOPE), k_pe, zeros(LANES - MLA_NOPE - MLA_ROPE)], -1)
    wnat = jnp.concatenate([a_k, b_k, c_q, c_kv, k_pe_pad], -1).astype(BF16)
    wtr = jnp.swapaxes(jnp.concatenate([a_q, a_v, b_q, b_v], -1), 1, 2).astype(BF16)
    uq = mla_w_uq.reshape(n_l, MLA_Q_RANK, MLA_HEADS, MLA_NOPE + MLA_ROPE)
    uq = jnp.pad(uq, ((0, 0), (0, 0), (0, 0), (0, LANES - MLA_NOPE - MLA_ROPE)))
    wuq = jnp.swapaxes(uq.reshape(n_l, MLA_Q_RANK, MLA_HEADS * LANES), 1, 2).astype(BF16)
    ukv = mla_w_ukv.reshape(n_l, MLA_KV_RANK, MLA_HEADS, MLA_NOPE + MLA_V)
    uk = jnp.pad(ukv[..., :MLA_NOPE], ((0, 0), (0, 0), (0, 0), (0, LANES - MLA_NOPE)))
    wuk = uk.reshape(n_l, MLA_KV_RANK, MLA_HEADS * LANES).astype(BF16)
    wuv = jnp.swapaxes(ukv[..., MLA_NOPE:].reshape(n_l, MLA_KV_RANK, MLA_HEADS * MLA_V), 1, 2).astype(BF16)
    return wnat, wtr, wuq, wuk, wuv


def kernel(x, p, positions, norm_ffn1, w_ffn1_gu, w_ffn1_down, norm_mix, w_in, mla_q_norm, mla_w_uq,
           mla_kv_norm, mla_w_ukv, diff_lambda_q1, diff_lambda_k1, diff_lambda_q2, diff_lambda_k2,
           diff_subln, w_out, norm_ffn2, w_ffn2_gu, w_ffn2_down, norm_ple, w_ple_gate, w_ple_proj,
           norm_final):
    batch, seq, _ = x.shape
    depth = w_in.shape[0]
    assert seq % ROW_TILE == 0 and ROW_TILE % ATT_TILE == 0
    t = batch * seq
    row3 = lambda a: a.reshape(a.shape[0], 1, a.shape[1])

    tabn, tabt = _rope_tables(positions)
    wnat, wtr, wuq, wuk, wuv = _prep_mixer_weights(w_in, mla_w_uq, mla_w_ukv)
    w1_gu, w1_d = w_ffn1_gu.astype(BF16), w_ffn1_down.astype(BF16)
    w2_gu, w2_d = w_ffn2_gu.astype(BF16), w_ffn2_down.astype(BF16)
    w_out_b = w_out.astype(BF16)
    w_gate_b, w_pproj_b = w_ple_gate.astype(BF16), w_ple_proj.astype(BF16)
    g_ffn1, g_mix, g_ffn2, g_ple = row3(norm_ffn1), row3(norm_mix), row3(norm_ffn2), row3(norm_ple)
    g_q, g_kv = row3(mla_q_norm), row3(mla_kv_norm)
    lq1, lk1, lq2, lk2 = (row3(a) for a in (diff_lambda_q1, diff_lambda_k1, diff_lambda_q2, diff_lambda_k2))
    subln4 = row3(jnp.tile(diff_subln, (1, DIFF_HEADS)))
    p_rows = p.reshape(depth, t, PLE_DIM)
    fin = norm_final.reshape(1, D_MODEL)

    h = x.reshape(t, D_MODEL)
    for layer in range(depth):
        h = _ffn(h, g_ffn1, w1_gu, w1_d, layer)
        qta, ka, vta, qtb, kb, vtb, qtc, kc, vtc = _proj(
            h, g_mix, wnat, wtr, g_q, g_kv, wuq, wuk, wuv, tabn, tabt, layer, batch, seq)
        mixed = _mixer_attn(lq1, lk1, lq2, lk2, subln4, qta, ka, vta, qtb, kb, vtb, qtc, kc, vtc,
                            layer, batch, seq)
        h = _out_proj(h, mixed.reshape(t, D_MODEL), w_out_b, layer)
        h = _ffn(h, g_ffn2, w2_gu, w2_d, layer)
        h = _ple(h, p_rows, g_ple, w_gate_b, w_pproj_b, fin, layer, final=(layer == depth - 1))
    return h.reshape(batch, seq, D_MODEL)
```

```python
import functools
import math

import jax
import jax.numpy as jnp
from jax import lax
from jax.experimental import pallas as pl
from jax.experimental.pallas import tpu as pltpu

F32 = jnp.float32
BF16 = jnp.bfloat16

D_MODEL = 1024
SB_HEADS, SB_DIM = 4, 64
DIFF_HEADS, DIFF_QK, DIFF_V = 4, 64, 128
MLA_HEADS, MLA_NOPE, MLA_ROPE, MLA_V = 4, 64, 32, 64
MLA_Q_RANK, MLA_KV_RANK = 256, 128
D_FF = 2816
PLE_DIM = 256
ROPE_THETA = 500000.0
DIFF_ROT_HALF = 8
MLA_ROT_HALF = 16
EPS = 1e-6
LOG2E = 1.4426950408889634
MASK_VALUE = -1e30
SHIFT_OFFSET = 60.0
SHIFT_LIMIT = 90.0
SHIFT_MARGIN = 1.0 + 2.0 ** -6

LANES = 128
MXU_DIM = 256
VMEM_LIMIT_BYTES = 56 * 1024 * 1024

ROW_TILE = 512
FF_CHUNK = 2 * MXU_DIM
ATT_TILE = 2 * MXU_DIM
V_PAD = 16
DIFF_VT = DIFF_V + V_PAD
MLA_VT = MLA_V + V_PAD

_SPLITS = (256, 256, 256, 512, 512, 512, MLA_Q_RANK, MLA_KV_RANK, MLA_ROPE)
_OFFS = tuple(int(sum(_SPLITS[:i])) for i in range(len(_SPLITS) + 1))
N_NAT = 256 + 512 + MLA_Q_RANK + MLA_KV_RANK + LANES
N_TR = 256 + 256 + 512 + 512


def _params(sem):
    return pltpu.CompilerParams(dimension_semantics=sem, vmem_limit_bytes=VMEM_LIMIT_BYTES)


def _rms_norm(x, gain):
    ms = jnp.mean(x * x, axis=-1, keepdims=True)
    return x * lax.rsqrt(ms + EPS) * gain


def _dot(a, b):
    return jnp.dot(a, b, preferred_element_type=F32)


def _dot_nt(a, b):
    return lax.dot_general(a, b, (((1,), (1,)), ((), ())), preferred_element_type=F32)


def _ffn_kernel(h_ref, gain_ref, wg_ref, wv_ref, wd_ref, o_ref):
    h = h_ref[...]
    xn = _rms_norm(h, gain_ref[...]).astype(BF16)
    acc = None
    for lo in range(0, D_FF, FF_CHUNK):
        hi = min(lo + FF_CHUNK, D_FF)
        g = _dot(xn, wg_ref[:, lo:hi])
        v = _dot(xn, wv_ref[:, lo:hi])
        a = (g * jax.nn.sigmoid(g) * v).astype(BF16)
        part = _dot(a, wd_ref[lo:hi, :])
        acc = part if acc is None else acc + part
    o_ref[...] = h + 0.5 * acc


def _ffn(h, gain, w_gu, w_down, layer):
    t = h.shape[0]
    resident = pl.Buffered(1)
    return pl.pallas_call(
        _ffn_kernel,
        grid=(t // ROW_TILE,),
        in_specs=[
            pl.BlockSpec((ROW_TILE, D_MODEL), lambda i: (i, 0)),
            pl.BlockSpec((None, 1, D_MODEL), lambda i: (layer, 0, 0)),
            pl.BlockSpec((None, D_MODEL, D_FF), lambda i: (layer, 0, 0), pipeline_mode=resident),
            pl.BlockSpec((None, D_MODEL, D_FF), lambda i: (layer, 0, 1), pipeline_mode=resident),
            pl.BlockSpec((None, D_FF, D_MODEL), lambda i: (layer, 0, 0), pipeline_mode=resident),
        ],
        out_specs=pl.BlockSpec((ROW_TILE, D_MODEL), lambda i: (i, 0)),
        out_shape=jax.ShapeDtypeStruct((t, D_MODEL), F32),
        compiler_params=_params(("parallel",)),
        name="ffn",
    )(h, gain, w_gu, w_gu, w_down)


def _rope_lanes(x, c, sp, sm, shift):
    return x * c + pltpu.roll(x, shift, 1) * sp + pltpu.roll(x, LANES - shift, 1) * sm


def _rope_rows(x, base, half, cos, sin):
    x1 = x[base:base + half]
    x2 = x[base + half:base + 2 * half]
    return x1 * cos - x2 * sin, x2 * cos + x1 * sin


def _proj_kernel(h_ref, gain_ref, wnat_ref, wtr_ref, qn_ref, kvn_ref, wuq_ref, wuk_ref, wuv_ref,
                 tabn_ref, tabt_ref,
                 qta_ref, ka_ref, vta_ref, qtb_ref, kb_ref, vtb_ref, qtc_ref, kc_ref, vtc_ref):
    n_sub = ROW_TILE // ATT_TILE
    u = _rms_norm(h_ref[...], gain_ref[...]).astype(BF16)
    nat = _dot(u, wnat_ref[...])
    tr = _dot_nt(wtr_ref[...], u)

    tabn = tabn_ref[...]
    tabt = tabt_ref[0]
    cos_d, sin_d = tabt[0:8], tabt[8:16]
    cos_m, sin_m = tabt[16:32], tabt[32:48]

    def put_rows(ref, val):
        for c in range(n_sub):
            ref[0, c] = val[c * ATT_TILE:(c + 1) * ATT_TILE].astype(ref.dtype)

    def put_cols(ref, val):
        for c in range(n_sub):
            ref[0, c] = val[:, c * ATT_TILE:(c + 1) * ATT_TILE].astype(ref.dtype)

    tm = h_ref.shape[0]
    ones_row = jnp.where(lax.broadcasted_iota(jnp.int32, (V_PAD, tm), 0) == 0, 1.0, 0.0)

    def with_ones(vt, n_heads, dv):
        parts = []
        for hd in range(n_heads):
            parts += [vt[hd * dv:(hd + 1) * dv], ones_row]
        return jnp.concatenate(parts, axis=0)

    qta_ref[0] = (tr[0:256] * (SB_DIM ** -0.5 * LOG2E)).astype(BF16)
    put_rows(ka_ref, nat[:, 0:256])
    put_cols(vta_ref, tr[256:512])

    qb = tr[512:1024]
    parts = []
    for ch in range(2 * DIFF_HEADS):
        base = ch * DIFF_QK
        r1, r2 = _rope_rows(qb, base, DIFF_ROT_HALF, cos_d, sin_d)
        parts += [r1, r2, qb[base + 2 * DIFF_ROT_HALF:base + DIFF_QK]]
    qtb_ref[0] = (jnp.concatenate(parts, axis=0) * (DIFF_QK ** -0.5 * LOG2E)).astype(BF16)
    kb = nat[:, 256:768]
    c_d, sp_d, sm_d = tabn[:, 0:128], tabn[:, 128:256], tabn[:, 256:384]
    kb = jnp.concatenate(
        [_rope_lanes(kb[:, c * LANES:(c + 1) * LANES], c_d, sp_d, sm_d, DIFF_ROT_HALF) for c in range(4)], axis=1)
    put_rows(kb_ref, kb)
    put_cols(vtb_ref, with_ones(tr[1024:1536], DIFF_HEADS, DIFF_V))

    cq = _rms_norm(nat[:, 768:1024], qn_ref[...]).astype(BF16)
    qc = _dot_nt(wuq_ref[...], cq)
    parts = []
    for hd in range(MLA_HEADS):
        base = hd * LANES
        r1, r2 = _rope_rows(qc, base + MLA_NOPE, MLA_ROT_HALF, cos_m, sin_m)
        parts += [qc[base:base + MLA_NOPE], r1, r2, qc[base + MLA_NOPE + MLA_ROPE:base + LANES]]
    qtc_ref[0] = (jnp.concatenate(parts, axis=0) * ((MLA_NOPE + MLA_ROPE) ** -0.5 * LOG2E)).astype(BF16)
    ckv = _rms_norm(nat[:, 1024:1152], kvn_ref[...]).astype(BF16)
    k_nope = _dot(ckv, wuk_ref[...])
    c_m, sp_m, sm_m = tabn[:, 384:512], tabn[:, 512:640], tabn[:, 640:768]
    k_rot = _rope_lanes(nat[:, 1152:1280], c_m, sp_m, sm_m, MLA_ROT_HALF)
    kc = jnp.concatenate([k_nope[:, hd * LANES:(hd + 1) * LANES] + k_rot for hd in range(MLA_HEADS)], axis=1)
    put_rows(kc_ref, kc)
    put_cols(vtc_ref, with_ones(_dot_nt(wuv_ref[...], ckv), MLA_HEADS, MLA_V))


def _proj(h, gain, wnat, wtr, qn, kvn, wuq, wuk, wuv, tabn, tabt, layer, batch, seq):
    t = h.shape[0]
    nblk = seq // ROW_TILE
    n_sub = ROW_TILE // ATT_TILE
    nkv = seq // ATT_TILE
    grid = (batch, nblk)

    def row_blocked(n):
        return (jax.ShapeDtypeStruct((batch, nkv, ATT_TILE, n), BF16),
                pl.BlockSpec((1, n_sub, ATT_TILE, n), lambda b, i: (b, i, 0, 0)))

    def col_blocked(n):
        return (jax.ShapeDtypeStruct((batch, nkv, n, ATT_TILE), BF16),
                pl.BlockSpec((1, n_sub, n, ATT_TILE), lambda b, i: (b, i, 0, 0)))

    def q_t(n):
        return (jax.ShapeDtypeStruct((batch, n, seq), BF16),
                pl.BlockSpec((1, n, ROW_TILE), lambda b, i: (b, 0, i)))

    outs = [q_t(256), row_blocked(256), col_blocked(256),
            q_t(512), row_blocked(512), col_blocked(DIFF_HEADS * DIFF_VT),
            q_t(512), row_blocked(512), col_blocked(MLA_HEADS * MLA_VT)]
    whole = lambda shape: pl.BlockSpec((None,) + shape, lambda b, i: (layer,) + (0,) * len(shape))
    return pl.pallas_call(
        _proj_kernel,
        grid=grid,
        in_specs=[
            pl.BlockSpec((ROW_TILE, D_MODEL), lambda b, i: (b * nblk + i, 0)),
            whole((1, D_MODEL)),
            whole((D_MODEL, N_NAT)),
            whole((N_TR, D_MODEL)),
            whole((1, MLA_Q_RANK)),
            whole((1, MLA_KV_RANK)),
            whole((4 * LANES, MLA_Q_RANK)),
            whole((MLA_KV_RANK, 4 * LANES)),
            whole((MLA_HEADS * MLA_V, MLA_KV_RANK)),
            pl.BlockSpec((ROW_TILE, 6 * LANES), lambda b, i: (b * nblk + i, 0)),
            pl.BlockSpec((1, 48, ROW_TILE), lambda b, i: (b, 0, i)),
        ],
        out_specs=[o[1] for o in outs],
        out_shape=[o[0] for o in outs],
        compiler_params=_params(("parallel", "parallel")),
        name="proj",
    )(h, gain, wnat, wtr, qn, kvn, wuq, wuk, wuv, tabn, tabt)


def _row_mask(n_rows, lo, hi, n_cols):
    r = lax.broadcasted_iota(jnp.int32, (n_rows, n_cols), 0)
    return jnp.where((r >= lo) & (r < hi), 1.0, 0.0).astype(BF16)


def _causal_softmax_sweep(i, q_heads, k_ref, vt_ref, m_ref, acc_ref, vt_rows):
    ta = ATT_TILE
    n = q_heads[0].shape[1]
    kk = lax.broadcasted_iota(jnp.int32, (ta, n), 0)
    qq = lax.broadcasted_iota(jnp.int32, (ta, n), 1) & (ta - 1)
    visible = kk <= qq

    def block(j, diag):
        k_blk = k_ref[0, j]
        vt_blk = vt_ref[0, j]
        n_heads = len(q_heads)
        st = [dict() for _ in range(n_heads)]

        def scores(hd):
            s = _dot(k_blk[:, hd * LANES:(hd + 1) * LANES], q_heads[hd])
            st[hd]["s"] = jnp.where(visible, s, MASK_VALUE) if diag else s

        def probs(hd):
            s = st[hd].pop("s")
            m_old = m_ref[hd]
            m_new = jnp.maximum(m_old, jnp.max(s, axis=0, keepdims=True))
            st[hd]["alpha"] = jnp.exp2(m_old - m_new)
            st[hd]["p"] = jnp.exp2(s - m_new).astype(BF16)
            m_ref[hd] = m_new

        def accumulate(hd):
            pv = _dot(vt_blk[hd * vt_rows:(hd + 1) * vt_rows], st[hd].pop("p"))
            acc_ref[hd] = st[hd].pop("alpha") * acc_ref[hd] + pv

        stages = (scores, probs, accumulate)
        for step in range(n_heads + len(stages) - 1):
            for s_idx in reversed(range(len(stages))):
                hd = step - s_idx
                if 0 <= hd < n_heads:
                    stages[s_idx](hd)

    def body(j, carry):
        block(j, False)
        return carry

    lax.fori_loop(0, i, body, 0)
    block(i, True)


def _diff_attn_kernel(lq1_ref, lk1_ref, lq2_ref, lk2_ref, subln_ref, qt_ref, k_ref, vt_ref, o_ref,
                      m_ref, acc_ref, *, lambda_init):
    i = pl.program_id(1)
    ta = ATT_TILE
    m_ref[...] = jnp.full_like(m_ref, MASK_VALUE)
    acc_ref[...] = jnp.zeros_like(acc_ref)

    qt = qt_ref[0]
    q_heads = []
    for hd in range(DIFF_HEADS):
        qh = qt[hd * LANES:(hd + 1) * LANES]
        q_heads.append(jnp.concatenate(
            [qh * _row_mask(LANES, 0, DIFF_QK, ta), qh * _row_mask(LANES, DIFF_QK, LANES, ta)], axis=1))

    _causal_softmax_sweep(i, q_heads, k_ref, vt_ref, m_ref, acc_ref, DIFF_VT)

    lam = (jnp.exp(jnp.sum(lq1_ref[...] * lk1_ref[...], axis=-1, keepdims=True))
           - jnp.exp(jnp.sum(lq2_ref[...] * lk2_ref[...], axis=-1, keepdims=True)) + lambda_init)
    outs = []
    for hd in range(DIFF_HEADS):
        acc = acc_ref[hd]
        o12 = acc[0:DIFF_V] / acc[DIFF_V:DIFF_V + 1]
        o = o12[:, 0:ta] - lam * o12[:, ta:2 * ta]
        ms = jnp.mean(o * o, axis=0, keepdims=True)
        outs.append(o * lax.rsqrt(ms + EPS))
    o_nat = jnp.concatenate(outs, axis=0).T
    o_ref[0] = (o_nat * subln_ref[...] * (1.0 - lambda_init)).astype(o_ref.dtype)


def _diff_attn(lq1, lk1, lq2, lk2, subln4, qt, k, vt, layer, batch, seq):
    nq = seq // ATT_TILE
    lambda_init = 0.8 - 0.6 * math.exp(-0.3 * layer)
    vec = lambda n: pl.BlockSpec((None, 1, n), lambda b, i: (layer, 0, 0))
    return pl.pallas_call(
        functools.partial(_diff_attn_kernel, lambda_init=lambda_init),
        grid=(batch, nq),
        in_specs=[
            vec(DIFF_QK), vec(DIFF_QK), vec(DIFF_QK), vec(DIFF_QK), vec(DIFF_HEADS * DIFF_V),
            pl.BlockSpec((1, 512, ATT_TILE), lambda b, i: (b, 0, i)),
            pl.BlockSpec((1, nq, ATT_TILE, 512), lambda b, i: (b, 0, 0, 0)),
            pl.BlockSpec((1, nq, DIFF_HEADS * DIFF_VT, ATT_TILE), lambda b, i: (b, 0, 0, 0)),
        ],
        out_specs=pl.BlockSpec((1, ATT_TILE, 512), lambda b, i: (b, i, 0)),
        out_shape=jax.ShapeDtypeStruct((batch, seq, 512), BF16),
        scratch_shapes=[pltpu.VMEM((DIFF_HEADS, 1, 2 * ATT_TILE), F32),
                        pltpu.VMEM((DIFF_HEADS, DIFF_VT, 2 * ATT_TILE), F32)],
        compiler_params=_params(("parallel", "arbitrary")),
        name="diff_attn",
    )(lq1, lk1, lq2, lk2, subln4, qt, k, vt)


def _mla_attn_kernel(qt_ref, k_ref, vt_ref, o_ref, m_ref, acc_ref):
    i = pl.program_id(1)
    m_ref[...] = jnp.full_like(m_ref, MASK_VALUE)
    acc_ref[...] = jnp.zeros_like(acc_ref)
    qt = qt_ref[0]
    q_heads = [qt[hd * LANES:(hd + 1) * LANES] for hd in range(MLA_HEADS)]
    _causal_softmax_sweep(i, q_heads, k_ref, vt_ref, m_ref, acc_ref, MLA_VT)
    outs = []
    for hd in range(MLA_HEADS):
        acc = acc_ref[hd]
        outs.append(acc[0:MLA_V] / acc[MLA_V:MLA_V + 1])
    o_ref[0] = jnp.concatenate(outs, axis=0).T.astype(o_ref.dtype)


def _mla_attn(qt, k, vt, batch, seq):
    nq = seq // ATT_TILE
    return pl.pallas_call(
        _mla_attn_kernel,
        grid=(batch, nq),
        in_specs=[
            pl.BlockSpec((1, 512, ATT_TILE), lambda b, i: (b, 0, i)),
            pl.BlockSpec((1, nq, ATT_TILE, 512), lambda b, i: (b, 0, 0, 0)),
            pl.BlockSpec((1, nq, MLA_HEADS * MLA_VT, ATT_TILE), lambda b, i: (b, 0, 0, 0)),
        ],
        out_specs=pl.BlockSpec((1, ATT_TILE, 256), lambda b, i: (b, i, 0)),
        out_shape=jax.ShapeDtypeStruct((batch, seq, 256), BF16),
        scratch_shapes=[pltpu.VMEM((MLA_HEADS, 1, ATT_TILE), F32),
                        pltpu.VMEM((MLA_HEADS, MLA_VT, ATT_TILE), F32)],
        compiler_params=_params(("parallel", "arbitrary")),
        name="mla_attn",
    )(qt, k, vt)


def _sb_attn_kernel(qt_ref, k_ref, vt_ref, o_ref, c_ref, acc_ref):
    i = pl.program_id(1)
    ta = ATT_TILE
    c_ref[...] = jnp.zeros_like(c_ref)
    acc_ref[...] = jnp.zeros_like(acc_ref)
    qt = qt_ref[0]
    q_masked = []
    for hd in range(SB_HEADS):
        qh = qt[(hd // 2) * LANES:(hd // 2 + 1) * LANES]
        lo = (hd % 2) * SB_DIM
        q_masked.append(qh * _row_mask(LANES, lo, lo + SB_DIM, ta))
    valid = (lax.broadcasted_iota(jnp.int32, (ta, ta), 0)
             < lax.broadcasted_iota(jnp.int32, (ta, ta), 1))
    half = MXU_DIM
    upper = jnp.where(lax.broadcasted_iota(jnp.int32, (half, half), 1)
                      > lax.broadcasted_iota(jnp.int32, (half, half), 0), 1.0, 0.0).astype(BF16)

    sign_bit = jnp.uint32(0x80000000)

    def block(j, diag):
        k_blk = k_ref[0, j]
        vt_blk = vt_ref[0, j]
        st = [dict() for _ in range(SB_HEADS)]

        def logits(hd):
            kh = k_blk[:, (hd // 2) * LANES:(hd // 2 + 1) * LANES]
            st[hd]["y"] = _dot(kh, q_masked[hd])

        def log_terms(hd):
            y = st[hd].pop("y")
            neg_abs = pltpu.bitcast(pltpu.bitcast(y, jnp.uint32) | sign_bit, F32)
            sp = jnp.log2(1.0 + jnp.exp2(neg_abs))
            log_beta = jnp.minimum(y, 0.0) - sp
            log_keep = log_beta - y
            if diag:
                log_keep = jnp.where(valid, log_keep, 0.0)
            st[hd].update(log_beta=log_beta, keep=log_keep.astype(BF16),
                          first=log_keep[0:1], mid=log_keep[half:half + 1])

        def suffix_sums(hd):
            keep = st[hd].pop("keep")
            st[hd]["later"] = _dot(upper, keep[half:])
            st[hd]["earlier"] = _dot(upper, keep[:half])

        def weights(hd):
            later, earlier = st[hd].pop("later"), st[hd].pop("earlier")
            c_old = c_ref[hd]
            later_total = later[0:1] + st[hd].pop("mid") + c_old
            between = jnp.concatenate([earlier + later_total, later + c_old], axis=0)
            a = jnp.exp2(st[hd].pop("log_beta") + between)
            if diag:
                a = jnp.where(valid, a, 0.0)
            c_ref[hd] = (earlier[0:1] + st[hd].pop("first")) + later_total
            st[hd]["a"] = a.astype(BF16)

        def accumulate(hd):
            acc_ref[hd] += _dot(vt_blk[hd * SB_DIM:(hd + 1) * SB_DIM], st[hd].pop("a"))

        stages = (logits, log_terms, suffix_sums, weights, accumulate)
        for step in range(SB_HEADS + len(stages) - 1):
            for s_idx in reversed(range(len(stages))):
                hd = step - s_idx
                if 0 <= hd < SB_HEADS:
                    stages[s_idx](hd)

    block(i, True)

    def body(jj, carry):
        block(i - 1 - jj, False)
        return carry

    lax.fori_loop(0, i, body, 0)
    o_ref[0] = acc_ref[...].reshape(SB_HEADS * SB_DIM, ta).T.astype(o_ref.dtype)


def _sb_attn(qt, k, vt, batch, seq):
    nq = seq // ATT_TILE
    return pl.pallas_call(
        _sb_attn_kernel,
        grid=(batch, nq),
        in_specs=[
            pl.BlockSpec((1, 256, ATT_TILE), lambda b, i: (b, 0, i)),
            pl.BlockSpec((1, nq, ATT_TILE, 256), lambda b, i: (b, 0, 0, 0)),
            pl.BlockSpec((1, nq, 256, ATT_TILE), lambda b, i: (b, 0, 0, 0)),
        ],
        out_specs=pl.BlockSpec((1, ATT_TILE, 256), lambda b, i: (b, i, 0)),
        out_shape=jax.ShapeDtypeStruct((batch, seq, 256), BF16),
        scratch_shapes=[pltpu.VMEM((4, 1, ATT_TILE), F32), pltpu.VMEM((4, SB_DIM, ATT_TILE), F32)],
        compiler_params=_params(("parallel", "arbitrary")),
        name="sb_attn",
    )(qt, k, vt)


def _emit_skewed(items):
    n_steps = max(k + len(stages) for k, stages in enumerate(items))
    for step in range(n_steps):
        for k, stages in enumerate(items):
            s_idx = step - k
            if 0 <= s_idx < len(stages):
                stages[s_idx]()


def _softmax_items(k_blk, vt_blk, q_heads, kv_heads, m_ref, acc_ref, vt_rows, visible):
    items = []
    for hd in range(len(q_heads)):
        st = {}
        kv = kv_heads[hd]

        def scores(hd=hd, st=st, kv=kv):
            s = _dot(k_blk[:, kv * LANES:(kv + 1) * LANES], q_heads[hd])
            st["s"] = s if visible is None else jnp.where(visible, s, MASK_VALUE)

        def probs(hd=hd, st=st):
            s = st.pop("s")
            m_old = m_ref[hd]
            m_new = jnp.maximum(m_old, jnp.max(s, axis=0, keepdims=True))
            st["alpha"] = jnp.exp2(m_old - m_new)
            st["p"] = jnp.exp2(s - m_new).astype(BF16)
            m_ref[hd] = m_new

        def accumulate(hd=hd, st=st, kv=kv):
            pv = _dot(vt_blk[kv * vt_rows:(kv + 1) * vt_rows], st.pop("p"))
            acc_ref[hd] = st.pop("alpha") * acc_ref[hd] + pv

        items.append([scores, probs, accumulate])
    return items


def _shifted_softmax_items(k_blk, vt_blk, q_aug, kv_heads, acc_ref, vt_rows, ones_col, visible):
    items = []
    for hd in range(len(q_aug)):
        st = {}
        kv = kv_heads[hd]

        def scores(hd=hd, st=st, kv=kv):
            k_aug = jnp.concatenate([k_blk[:, kv * LANES:(kv + 1) * LANES], ones_col], axis=1)
            s = _dot(k_aug, q_aug[hd])
            st["s"] = s if visible is None else jnp.where(visible, s, MASK_VALUE)

        def probs(st=st):
            st["p"] = jnp.exp2(st.pop("s")).astype(BF16)

        def accumulate(hd=hd, st=st, kv=kv):
            acc_ref[hd] += _dot(vt_blk[kv * vt_rows:(kv + 1) * vt_rows], st.pop("p"))

        items.append([scores, probs, accumulate])
    return items


def _stick_breaking_items(k_blk, vt_blk, q_heads, c_ref, acc_ref, upper, valid):
    half = MXU_DIM
    sign_bit = jnp.uint32(0x80000000)
    items = []
    for hd in range(len(q_heads)):
        st = {}

        def logits(hd=hd, st=st):
            st["y"] = _dot(k_blk[:, (hd // 2) * LANES:(hd // 2 + 1) * LANES], q_heads[hd])

        def log_terms(hd=hd, st=st):
            y = st.pop("y")
            neg_abs = pltpu.bitcast(pltpu.bitcast(y, jnp.uint32) | sign_bit, F32)
            sp = jnp.log2(1.0 + jnp.exp2(neg_abs))
            log_beta = jnp.minimum(y, 0.0) - sp
            log_keep = log_beta - y
            if valid is not None:
                log_keep = jnp.where(valid, log_keep, 0.0)
            st.update(log_beta=log_beta, keep=log_keep.astype(BF16),
                      first=log_keep[0:1], mid=log_keep[half:half + 1])

        def suffix_sums(hd=hd, st=st):
            keep = st.pop("keep")
            st["later"] = _dot(upper, keep[half:])
            st["earlier"] = _dot(upper, keep[:half])

        def weights(hd=hd, st=st):
            later, earlier = st.pop("later"), st.pop("earlier")
            c_old = c_ref[hd]
            later_total = later[0:1] + st.pop("mid") + c_old
            between = jnp.concatenate([earlier + later_total, later + c_old], axis=0)
            a = jnp.exp2(st.pop("log_beta") + between)
            if valid is not None:
                a = jnp.where(valid, a, 0.0)
            c_ref[hd] = (earlier[0:1] + st.pop("first")) + later_total
            st["a"] = a.astype(BF16)

        def accumulate(hd=hd, st=st):
            acc_ref[hd] += _dot(vt_blk[hd * SB_DIM:(hd + 1) * SB_DIM], st.pop("a"))

        items.append([logits, log_terms, suffix_sums, weights, accumulate])
    return items


def _mixer_attn_kernel(lq1_ref, lk1_ref, lq2_ref, lk2_ref, subln_ref,
                       qta_ref, ka_ref, vta_ref, qtb_ref, kb_ref, vtb_ref, qtc_ref, kc_ref, vtc_ref,
                       *refs, lambda_init, shifted, nq):
    i = pl.program_id(1)
    ta = ATT_TILE
    if shifted:
        o_ref, worst_ref, c_ref, acc_a_ref, acc_b_ref, acc_c_ref, kmax_b_ref, kmax_c_ref = refs
    else:
        o_ref, c_ref, acc_a_ref, m_b_ref, acc_b_ref, m_c_ref, acc_c_ref = refs
        m_b_ref[...] = jnp.full_like(m_b_ref, MASK_VALUE)
        m_c_ref[...] = jnp.full_like(m_c_ref, MASK_VALUE)
    c_ref[...] = jnp.zeros_like(c_ref)
    acc_a_ref[...] = jnp.zeros_like(acc_a_ref)
    acc_b_ref[...] = jnp.zeros_like(acc_b_ref)
    acc_c_ref[...] = jnp.zeros_like(acc_c_ref)

    qta, qtb, qtc = qta_ref[0], qtb_ref[0], qtc_ref[0]
    q_a, q_b, q_c = [], [], []
    for hd in range(SB_HEADS):
        lo = (hd % 2) * SB_DIM
        q_a.append(qta[(hd // 2) * LANES:(hd // 2 + 1) * LANES] * _row_mask(LANES, lo, lo + SB_DIM, ta))
    for hd in range(DIFF_HEADS):
        qh = qtb[hd * LANES:(hd + 1) * LANES]
        q_b += [qh * _row_mask(LANES, 0, DIFF_QK, ta), qh * _row_mask(LANES, DIFF_QK, LANES, ta)]
    for hd in range(MLA_HEADS):
        q_c.append(qtc[hd * LANES:(hd + 1) * LANES])

    if shifted:
        @pl.when(i == 0)
        def _():
            for k_ref, kmax_ref in ((kb_ref, kmax_b_ref), (kc_ref, kmax_c_ref)):
                kmax = jnp.zeros((1, k_ref.shape[-1]), F32)
                for jb in range(nq):
                    kmax = jnp.maximum(kmax, jnp.max(jnp.abs(k_ref[0, jb].astype(F32)), axis=0, keepdims=True))
                kmax_ref[...] = kmax

        ones_col = jnp.where(lax.broadcasted_iota(jnp.int32, (ta, LANES), 1) == 0, 1.0, 0.0).astype(BF16)
        first_row = lax.broadcasted_iota(jnp.int32, (LANES, ta), 0) == 0
        first_of_16 = lax.broadcasted_iota(jnp.int32, (16, LANES), 0) == 0
        worst = jnp.zeros((1, 1), F32)

        def augment(q, kmax_row):
            kmax_tile = jnp.where(first_of_16, kmax_row, 0.0).astype(BF16)
            bound = _dot(kmax_tile, jnp.abs(q))[0:1]
            shift = bound * SHIFT_MARGIN
            extra = jnp.where(first_row, SHIFT_OFFSET - shift, 0.0).astype(BF16)
            return jnp.concatenate([q, extra], axis=0), jnp.max(shift, axis=1, keepdims=True)

        qb_aug, qc_aug = [], []
        for n, q in enumerate(q_b):
            kv = n // 2
            q_aug, top = augment(q, kmax_b_ref[:, kv * LANES:(kv + 1) * LANES])
            qb_aug.append(q_aug)
            worst = jnp.maximum(worst, top)
        for n, q in enumerate(q_c):
            q_aug, top = augment(q, kmax_c_ref[:, n * LANES:(n + 1) * LANES])
            qc_aug.append(q_aug)
            worst = jnp.maximum(worst, top)
        worst_ref[...] = jnp.broadcast_to(worst, worst_ref.shape)
    kk = lax.broadcasted_iota(jnp.int32, (ta, ta), 0)
    qq = lax.broadcasted_iota(jnp.int32, (ta, ta), 1)
    visible = kk <= qq
    valid_a = kk < qq
    upper = jnp.where(lax.broadcasted_iota(jnp.int32, (MXU_DIM, MXU_DIM), 1)
                      > lax.broadcasted_iota(jnp.int32, (MXU_DIM, MXU_DIM), 0), 1.0, 0.0).astype(BF16)

    def block(j, diag):
        a_items = _stick_breaking_items(ka_ref[0, j], vta_ref[0, j], q_a, c_ref, acc_a_ref, upper,
                                        valid_a if diag else None)
        kv_b, kv_c = [n // 2 for n in range(2 * DIFF_HEADS)], list(range(MLA_HEADS))
        mask = visible if diag else None
        if shifted:
            b_items = _shifted_softmax_items(kb_ref[0, j], vtb_ref[0, j], qb_aug, kv_b, acc_b_ref, DIFF_VT,
                                             ones_col, mask)
            c_items = _shifted_softmax_items(kc_ref[0, j], vtc_ref[0, j], qc_aug, kv_c, acc_c_ref, MLA_VT,
                                             ones_col, mask)
        else:
            b_items = _softmax_items(kb_ref[0, j], vtb_ref[0, j], q_b, kv_b, m_b_ref, acc_b_ref, DIFF_VT, mask)
            c_items = _softmax_items(kc_ref[0, j], vtc_ref[0, j], q_c, kv_c, m_c_ref, acc_c_ref, MLA_VT, mask)
        items = []
        for hd in range(SB_HEADS):
            items += [a_items[hd], b_items[2 * hd], c_items[hd], b_items[2 * hd + 1]]
        _emit_skewed(items)

    block(i, True)

    def body(jj, carry):
        block(i - 1 - jj, False)
        return carry

    lax.fori_loop(0, i, body, 0)

    o_ref[0, :, 0:256] = acc_a_ref[...].reshape(SB_HEADS * SB_DIM, ta).T.astype(o_ref.dtype)

    lam = (jnp.exp(jnp.sum(lq1_ref[...] * lk1_ref[...], axis=-1, keepdims=True))
           - jnp.exp(jnp.sum(lq2_ref[...] * lk2_ref[...], axis=-1, keepdims=True)) + lambda_init)
    outs = []
    for hd in range(DIFF_HEADS):
        acc1, acc2 = acc_b_ref[2 * hd], acc_b_ref[2 * hd + 1]
        o = (acc1[0:DIFF_V] / acc1[DIFF_V:DIFF_V + 1]
             - lam * (acc2[0:DIFF_V] / acc2[DIFF_V:DIFF_V + 1]))
        ms = jnp.mean(o * o, axis=0, keepdims=True)
        outs.append(o * lax.rsqrt(ms + EPS))
    o_nat = jnp.concatenate(outs, axis=0).T
    o_ref[0, :, 256:768] = (o_nat * subln_ref[...] * (1.0 - lambda_init)).astype(o_ref.dtype)

    outs = []
    for hd in range(MLA_HEADS):
        acc = acc_c_ref[hd]
        outs.append(acc[0:MLA_V] / acc[MLA_V:MLA_V + 1])
    o_ref[0, :, 768:1024] = jnp.concatenate(outs, axis=0).T.astype(o_ref.dtype)


def _mixer_attn(lq1, lk1, lq2, lk2, subln4, qta, ka, vta, qtb, kb, vtb, qtc, kc, vtc, layer, batch, seq):
    nq = seq // ATT_TILE
    lambda_init = 0.8 - 0.6 * math.exp(-0.3 * layer)
    vec = lambda n: pl.BlockSpec((None, 1, n), lambda b, i: (layer, 0, 0))
    q_t = lambda n: pl.BlockSpec((1, n, ATT_TILE), lambda b, i: (b, 0, i))
    per_batch = lambda r, c: pl.BlockSpec((1, nq, r, c), lambda b, i: (b, 0, 0, 0),
                                          pipeline_mode=pl.Buffered(1))
    operands = (lq1, lk1, lq2, lk2, subln4, qta, ka, vta, qtb, kb, vtb, qtc, kc, vtc)
    in_specs = [
        vec(DIFF_QK), vec(DIFF_QK), vec(DIFF_QK), vec(DIFF_QK), vec(DIFF_HEADS * DIFF_V),
        q_t(256), per_batch(ATT_TILE, 256), per_batch(256, ATT_TILE),
        q_t(512), per_batch(ATT_TILE, 512), per_batch(DIFF_HEADS * DIFF_VT, ATT_TILE),
        q_t(512), per_batch(ATT_TILE, 512), per_batch(MLA_HEADS * MLA_VT, ATT_TILE),
    ]
    mixed_spec = pl.BlockSpec((1, ATT_TILE, D_MODEL), lambda b, i: (b, i, 0))
    mixed_shape = jax.ShapeDtypeStruct((batch, seq, D_MODEL), BF16)
    state = lambda n, rows: pltpu.VMEM((n, rows, ATT_TILE), F32)

    def call(shifted):
        if shifted:
            out_specs = [mixed_spec, pl.BlockSpec((1, 1, 8, LANES), lambda b, i: (b, i, 0, 0))]
            out_shape = [mixed_shape, jax.ShapeDtypeStruct((batch, nq, 8, LANES), F32)]
            scratch = [state(SB_HEADS, 1), state(SB_HEADS, SB_DIM), state(2 * DIFF_HEADS, DIFF_VT),
                       state(MLA_HEADS, MLA_VT), pltpu.VMEM((1, 512), F32), pltpu.VMEM((1, 512), F32)]
        else:
            out_specs, out_shape = mixed_spec, mixed_shape
            scratch = [state(SB_HEADS, 1), state(SB_HEADS, SB_DIM), state(2 * DIFF_HEADS, 1),
                       state(2 * DIFF_HEADS, DIFF_VT), state(MLA_HEADS, 1), state(MLA_HEADS, MLA_VT)]
        return pl.pallas_call(
            functools.partial(_mixer_attn_kernel, lambda_init=lambda_init, shifted=shifted, nq=nq),
            grid=(batch, nq),
            in_specs=in_specs,
            out_specs=out_specs,
            out_shape=out_shape,
            scratch_shapes=scratch,
            compiler_params=_params(("arbitrary", "arbitrary")),
            name="mixer_attn" if shifted else "mixer_attn_online",
        )(*operands)

    mixed, worst = call(True)
    return lax.cond(jnp.max(worst) > SHIFT_LIMIT, lambda: call(False), lambda: mixed)


def _out_proj_kernel(h_ref, mix_ref, w_ref, o_ref):
    o_ref[...] = h_ref[...] + _dot(mix_ref[...], w_ref[...])


def _out_proj(h, mixed, w_out, layer):
    t = h.shape[0]
    rows = lambda n: pl.BlockSpec((ROW_TILE, n), lambda i: (i, 0))
    return pl.pallas_call(
        _out_proj_kernel,
        grid=(t // ROW_TILE,),
        in_specs=[rows(D_MODEL), rows(D_MODEL),
                  pl.BlockSpec((None, D_MODEL, D_MODEL), lambda i: (layer, 0, 0))],
        out_specs=rows(D_MODEL),
        out_shape=jax.ShapeDtypeStruct((t, D_MODEL), F32),
        compiler_params=_params(("parallel",)),
        name="out_proj",
    )(h, mixed, w_out)


def _ple_kernel(h_ref, p_ref, gain_ref, wg_ref, wp_ref, fin_ref, o_ref, *, final):
    h = h_ref[...]
    gate = jax.nn.sigmoid(_dot(_rms_norm(h, gain_ref[...]).astype(BF16), wg_ref[...]))
    out = h + _dot(p_ref[0].astype(BF16), wp_ref[...]) * gate
    if final:
        out = _rms_norm(out, fin_ref[...])
    o_ref[...] = out


def _ple(h, p, gain, w_gate, w_proj, norm_final, layer, final):
    t = h.shape[0]
    return pl.pallas_call(
        functools.partial(_ple_kernel, final=final),
        grid=(t // ROW_TILE,),
        in_specs=[
            pl.BlockSpec((ROW_TILE, D_MODEL), lambda i: (i, 0)),
            pl.BlockSpec((1, ROW_TILE, PLE_DIM), lambda i: (layer, i, 0)),
            pl.BlockSpec((None, 1, D_MODEL), lambda i: (layer, 0, 0)),
            pl.BlockSpec((None, D_MODEL, D_MODEL), lambda i: (layer, 0, 0)),
            pl.BlockSpec((None, PLE_DIM, D_MODEL), lambda i: (layer, 0, 0)),
            pl.BlockSpec((1, D_MODEL), lambda i: (0, 0)),
        ],
        out_specs=pl.BlockSpec((ROW_TILE, D_MODEL), lambda i: (i, 0)),
        out_shape=jax.ShapeDtypeStruct((t, D_MODEL), F32),
        compiler_params=_params(("parallel",)),
        name="ple",
    )(h, p, gain, w_gate, w_proj, norm_final)


def _rope_tables(positions):
    b, s = positions.shape
    pos = positions.astype(F32)[..., None]

    def cs(rot_dim):
        inv_freq = 1.0 / (ROPE_THETA ** (jnp.arange(0, rot_dim, 2, dtype=F32) / rot_dim))
        ang = pos * inv_freq
        return jnp.cos(ang), jnp.sin(ang)

    cos_d, sin_d = cs(2 * DIFF_ROT_HALF)
    cos_m, sin_m = cs(2 * MLA_ROT_HALF)
    one = lambda n: jnp.ones((b, s, n), F32)
    zero = lambda n: jnp.zeros((b, s, n), F32)
    c_d = jnp.tile(jnp.concatenate([cos_d, cos_d, one(48)], -1), (1, 1, 2))
    sp_d = jnp.tile(jnp.concatenate([zero(8), sin_d, zero(48)], -1), (1, 1, 2))
    sm_d = jnp.tile(jnp.concatenate([-sin_d, zero(56)], -1), (1, 1, 2))
    c_m = jnp.concatenate([one(64), cos_m, cos_m, one(32)], -1)
    sp_m = jnp.concatenate([zero(80), sin_m, zero(32)], -1)
    sm_m = jnp.concatenate([zero(64), -sin_m, zero(48)], -1)
    tabn = jnp.concatenate([c_d, sp_d, sm_d, c_m, sp_m, sm_m], -1).reshape(b * s, 6 * LANES)
    tabt = jnp.swapaxes(jnp.concatenate([cos_d, sin_d, cos_m, sin_m], -1), 1, 2)
    return tabn, tabt


def _prep_mixer_weights(w_in, mla_w_uq, mla_w_ukv):
    n_l = w_in.shape[0]
    seg = lambda i: w_in[:, :, _OFFS[i]:_OFFS[i + 1]]
    a_q, a_k, a_v, b_q, b_k, b_v, c_q, c_kv, k_pe = (seg(i) for i in range(9))
    zeros = lambda n: jnp.zeros((n_l, D_MODEL, n), w_in.dtype)
    k_pe_pad = jnp.concatenate([zeros(MLA_NOPE), k_pe, zeros(LANES - MLA_NOPE - MLA_ROPE)], -1)
    wnat = jnp.concatenate([a_k, b_k, c_q, c_kv, k_pe_pad], -1).astype(BF16)
    wtr = jnp.swapaxes(jnp.concatenate([a_q, a_v, b_q, b_v], -1), 1, 2).astype(BF16)
    uq = mla_w_uq.reshape(n_l, MLA_Q_RANK, MLA_HEADS, MLA_NOPE + MLA_ROPE)
    uq = jnp.pad(uq, ((0, 0), (0, 0), (0, 0), (0, LANES - MLA_NOPE - MLA_ROPE)))
    wuq = jnp.swapaxes(uq.reshape(n_l, MLA_Q_RANK, MLA_HEADS * LANES), 1, 2).astype(BF16)
    ukv = mla_w_ukv.reshape(n_l, MLA_KV_RANK, MLA_HEADS, MLA_NOPE + MLA_V)
    uk = jnp.pad(ukv[..., :MLA_NOPE], ((0, 0), (0, 0), (0, 0), (0, LANES - MLA_NOPE)))
    wuk = uk.reshape(n_l, MLA_KV_RANK, MLA_HEADS * LANES).astype(BF16)
    wuv = jnp.swapaxes(ukv[..., MLA_NOPE:].reshape(n_l, MLA_KV_RANK, MLA_HEADS * MLA_V), 1, 2).astype(BF16)
    return wnat, wtr, wuq, wuk, wuv


def kernel(x, p, positions, norm_ffn1, w_ffn1_gu, w_ffn1_down, norm_mix, w_in, mla_q_norm, mla_w_uq,
           mla_kv_norm, mla_w_ukv, diff_lambda_q1, diff_lambda_k1, diff_lambda_q2, diff_lambda_k2,
           diff_subln, w_out, norm_ffn2, w_ffn2_gu, w_ffn2_down, norm_ple, w_ple_gate, w_ple_proj,
           norm_final):
    batch, seq, _ = x.shape
    depth = w_in.shape[0]
    assert seq % ROW_TILE == 0 and ROW_TILE % ATT_TILE == 0
    t = batch * seq
    row3 = lambda a: a.reshape(a.shape[0], 1, a.shape[1])

    tabn, tabt = _rope_tables(positions)
    wnat, wtr, wuq, wuk, wuv = _prep_mixer_weights(w_in, mla_w_uq, mla_w_ukv)
    w1_gu, w1_d = w_ffn1_gu.astype(BF16), w_ffn1_down.astype(BF16)
    w2_gu, w2_d = w_ffn2_gu.astype(BF16), w_ffn2_down.astype(BF16)
    w_out_b = w_out.astype(BF16)
    w_gate_b, w_pproj_b = w_ple_gate.astype(BF16), w_ple_proj.astype(BF16)
    g_ffn1, g_mix, g_ffn2, g_ple = row3(norm_ffn1), row3(norm_mix), row3(norm_ffn2), row3(norm_ple)
    g_q, g_kv = row3(mla_q_norm), row3(mla_kv_norm)
    lq1, lk1, lq2, lk2 = (row3(a) for a in (diff_lambda_q1, diff_lambda_k1, diff_lambda_q2, diff_lambda_k2))
    subln4 = row3(jnp.tile(diff_subln, (1, DIFF_HEADS)))
    p_rows = p.reshape(depth, t, PLE_DIM)
    fin = norm_final.reshape(1, D_MODEL)

    h = x.reshape(t, D_MODEL)
    for layer in range(depth):
        h = _ffn(h, g_ffn1, w1_gu, w1_d, layer)
        qta, ka, vta, qtb, kb, vtb, qtc, kc, vtc = _proj(
            h, g_mix, wnat, wtr, g_q, g_kv, wuq, wuk, wuv, tabn, tabt, layer, batch, seq)
        mixed = _mixer_attn(lq1, lk1, lq2, lk2, subln4, qta, ka, vta, qtb, kb, vtb, qtc, kc, vtc,
                            layer, batch, seq)
        h = _out_proj(h, mixed.reshape(t, D_MODEL), w_out_b, layer)
        h = _ffn(h, g_ffn2, w2_gu, w2_d, layer)
        h = _ple(h, p_rows, g_ple, w_gate_b, w_pproj_b, fin, layer, final=(layer == depth - 1))
    return h.reshape(batch, seq, D_MODEL)
```

```python
import functools
import math

import jax
import jax.numpy as jnp
from jax import lax
from jax.experimental import pallas as pl
from jax.experimental.pallas import tpu as pltpu

F32 = jnp.float32
BF16 = jnp.bfloat16

D_MODEL = 1024
SB_HEADS, SB_DIM = 4, 64
DIFF_HEADS, DIFF_QK, DIFF_V = 4, 64, 128
MLA_HEADS, MLA_NOPE, MLA_ROPE, MLA_V = 4, 64, 32, 64
MLA_Q_RANK, MLA_KV_RANK = 256, 128
D_FF = 2816
PLE_DIM = 256
ROPE_THETA = 500000.0
DIFF_ROT_HALF = 8
MLA_ROT_HALF = 16
EPS = 1e-6
LOG2E = 1.4426950408889634
MASK_VALUE = -1e30
SHIFT_OFFSET = 60.0
SHIFT_LIMIT = 90.0
SHIFT_MARGIN = 1.0 + 2.0 ** -6

LANES = 128
MXU_DIM = 256
VMEM_LIMIT_BYTES = 56 * 1024 * 1024

ROW_TILE = 512
FF_CHUNK = 2 * MXU_DIM
ATT_TILE = 2 * MXU_DIM
V_PAD = 16
DIFF_VT = DIFF_V + V_PAD
MLA_VT = MLA_V + V_PAD

_SPLITS = (256, 256, 256, 512, 512, 512, MLA_Q_RANK, MLA_KV_RANK, MLA_ROPE)
_OFFS = tuple(int(sum(_SPLITS[:i])) for i in range(len(_SPLITS) + 1))
N_NAT = 256 + 512 + MLA_Q_RANK + MLA_KV_RANK + LANES
N_TR = 256 + 256 + 512 + 512


def _params(sem):
    return pltpu.CompilerParams(dimension_semantics=sem, vmem_limit_bytes=VMEM_LIMIT_BYTES)


def _rms_norm(x, gain):
    ms = jnp.mean(x * x, axis=-1, keepdims=True)
    return x * lax.rsqrt(ms + EPS) * gain


def _dot(a, b):
    return jnp.dot(a, b, preferred_element_type=F32)


def _dot_nt(a, b):
    return lax.dot_general(a, b, (((1,), (1,)), ((), ())), preferred_element_type=F32)


def _half_step_ffn(h, gain, wg_ref, wv_ref, wd_ref):
    xn = _rms_norm(h, gain).astype(BF16)
    acc = None
    for lo in range(0, D_FF, FF_CHUNK):
        hi = min(lo + FF_CHUNK, D_FF)
        g = _dot(xn, wg_ref[:, lo:hi])
        v = _dot(xn, wv_ref[:, lo:hi])
        a = (g * jax.nn.sigmoid(g) * v).astype(BF16)
        part = _dot(a, wd_ref[lo:hi, :])
        acc = part if acc is None else acc + part
    return h + 0.5 * acc


def _rope_lanes(x, c, sp, sm, shift):
    return x * c + pltpu.roll(x, shift, 1) * sp + pltpu.roll(x, LANES - shift, 1) * sm


def _rope_rows(x, base, half, cos, sin):
    x1 = x[base:base + half]
    x2 = x[base + half:base + 2 * half]
    return x1 * cos - x2 * sin, x2 * cos + x1 * sin


def _pre_mixer_kernel(h_ref, g1_ref, wg_ref, wv_ref, wd_ref,
                      gain_ref, wnat_ref, wtr_ref, qn_ref, kvn_ref, wuq_ref, wuk_ref, wuv_ref,
                      tabn_ref, tabt_ref,
                      h_out_ref, qta_ref, ka_ref, vta_ref, qtb_ref, kb_ref, vtb_ref, qtc_ref, kc_ref, vtc_ref):
    h = _half_step_ffn(h_ref[...], g1_ref[...], wg_ref, wv_ref, wd_ref)
    h_out_ref[...] = h

    u = _rms_norm(h, gain_ref[...]).astype(BF16)
    nat = _dot(u, wnat_ref[...])
    tr = _dot_nt(wtr_ref[...], u)

    tabn = tabn_ref[...]
    tabt = tabt_ref[0]
    cos_d, sin_d = tabt[0:8], tabt[8:16]
    cos_m, sin_m = tabt[16:32], tabt[32:48]

    tm = h_ref.shape[0]
    ones_row = jnp.where(lax.broadcasted_iota(jnp.int32, (V_PAD, tm), 0) == 0, 1.0, 0.0)

    def with_ones(vt, n_heads, dv):
        parts = []
        for hd in range(n_heads):
            parts += [vt[hd * dv:(hd + 1) * dv], ones_row]
        return jnp.concatenate(parts, axis=0)

    qta_ref[0] = (tr[0:256] * (SB_DIM ** -0.5 * LOG2E)).astype(BF16)
    ka_ref[0, 0] = nat[:, 0:256].astype(BF16)
    vta_ref[0, 0] = tr[256:512].astype(BF16)

    qb = tr[512:1024]
    parts = []
    for ch in range(2 * DIFF_HEADS):
        base = ch * DIFF_QK
        r1, r2 = _rope_rows(qb, base, DIFF_ROT_HALF, cos_d, sin_d)
        parts += [r1, r2, qb[base + 2 * DIFF_ROT_HALF:base + DIFF_QK]]
    qtb_ref[0] = (jnp.concatenate(parts, axis=0) * (DIFF_QK ** -0.5 * LOG2E)).astype(BF16)
    kb = nat[:, 256:768]
    c_d, sp_d, sm_d = tabn[:, 0:128], tabn[:, 128:256], tabn[:, 256:384]
    kb = jnp.concatenate(
        [_rope_lanes(kb[:, c * LANES:(c + 1) * LANES], c_d, sp_d, sm_d, DIFF_ROT_HALF) for c in range(4)], axis=1)
    kb_ref[0, 0] = kb.astype(BF16)
    vtb_ref[0, 0] = with_ones(tr[1024:1536], DIFF_HEADS, DIFF_V).astype(BF16)

    cq = _rms_norm(nat[:, 768:1024], qn_ref[...]).astype(BF16)
    qc = _dot_nt(wuq_ref[...], cq)
    parts = []
    for hd in range(MLA_HEADS):
        base = hd * LANES
        r1, r2 = _rope_rows(qc, base + MLA_NOPE, MLA_ROT_HALF, cos_m, sin_m)
        parts += [qc[base:base + MLA_NOPE], r1, r2, qc[base + MLA_NOPE + MLA_ROPE:base + LANES]]
    qtc_ref[0] = (jnp.concatenate(parts, axis=0) * ((MLA_NOPE + MLA_ROPE) ** -0.5 * LOG2E)).astype(BF16)
    ckv = _rms_norm(nat[:, 1024:1152], kvn_ref[...]).astype(BF16)
    k_nope = _dot(ckv, wuk_ref[...])
    c_m, sp_m, sm_m = tabn[:, 384:512], tabn[:, 512:640], tabn[:, 640:768]
    k_rot = _rope_lanes(nat[:, 1152:1280], c_m, sp_m, sm_m, MLA_ROT_HALF)
    kc = jnp.concatenate([k_nope[:, hd * LANES:(hd + 1) * LANES] + k_rot for hd in range(MLA_HEADS)], axis=1)
    kc_ref[0, 0] = kc.astype(BF16)
    vtc_ref[0, 0] = with_ones(_dot_nt(wuv_ref[...], ckv), MLA_HEADS, MLA_V).astype(BF16)


def _pre_mixer(h, g1, w_gu, w_down, gain, wnat, wtr, qn, kvn, wuq, wuk, wuv, tabn, tabt, layer, batch, seq):
    t = h.shape[0]
    nblk = seq // ATT_TILE
    resident = pl.Buffered(1)

    def row_blocked(n):
        return (jax.ShapeDtypeStruct((batch, nblk, ATT_TILE, n), BF16),
                pl.BlockSpec((1, 1, ATT_TILE, n), lambda b, i: (b, i, 0, 0)))

    def col_blocked(n):
        return (jax.ShapeDtypeStruct((batch, nblk, n, ATT_TILE), BF16),
                pl.BlockSpec((1, 1, n, ATT_TILE), lambda b, i: (b, i, 0, 0)))

    def q_t(n):
        return (jax.ShapeDtypeStruct((batch, n, seq), BF16),
                pl.BlockSpec((1, n, ATT_TILE), lambda b, i: (b, 0, i)))

    rows = pl.BlockSpec((ATT_TILE, D_MODEL), lambda b, i: (b * nblk + i, 0))
    outs = [(jax.ShapeDtypeStruct((t, D_MODEL), F32), rows),
            q_t(256), row_blocked(256), col_blocked(256),
            q_t(512), row_blocked(512), col_blocked(DIFF_HEADS * DIFF_VT),
            q_t(512), row_blocked(512), col_blocked(MLA_HEADS * MLA_VT)]
    whole = lambda shape: pl.BlockSpec((None,) + shape, lambda b, i: (layer,) + (0,) * len(shape),
                                       pipeline_mode=resident)
    return pl.pallas_call(
        _pre_mixer_kernel,
        grid=(batch, nblk),
        in_specs=[
            rows,
            whole((1, D_MODEL)),
            pl.BlockSpec((None, D_MODEL, D_FF), lambda b, i: (layer, 0, 0), pipeline_mode=resident),
            pl.BlockSpec((None, D_MODEL, D_FF), lambda b, i: (layer, 0, 1), pipeline_mode=resident),
            whole((D_FF, D_MODEL)),
            whole((1, D_MODEL)),
            whole((D_MODEL, N_NAT)),
            whole((N_TR, D_MODEL)),
            whole((1, MLA_Q_RANK)),
            whole((1, MLA_KV_RANK)),
            whole((4 * LANES, MLA_Q_RANK)),
            whole((MLA_KV_RANK, 4 * LANES)),
            whole((MLA_HEADS * MLA_V, MLA_KV_RANK)),
            pl.BlockSpec((ATT_TILE, 6 * LANES), lambda b, i: (b * nblk + i, 0)),
            pl.BlockSpec((1, 48, ATT_TILE), lambda b, i: (b, 0, i)),
        ],
        out_specs=[o[1] for o in outs],
        out_shape=[o[0] for o in outs],
        compiler_params=_params(("parallel", "parallel")),
        name="pre_mixer",
    )(h, g1, w_gu, w_gu, w_down, gain, wnat, wtr, qn, kvn, wuq, wuk, wuv, tabn, tabt)


def _row_mask(n_rows, lo, hi, n_cols):
    r = lax.broadcasted_iota(jnp.int32, (n_rows, n_cols), 0)
    return jnp.where((r >= lo) & (r < hi), 1.0, 0.0).astype(BF16)


def _emit_skewed(items):
    n_steps = max(k + len(stages) for k, stages in enumerate(items))
    for step in range(n_steps):
        for k, stages in enumerate(items):
            s_idx = step - k
            if 0 <= s_idx < len(stages):
                stages[s_idx]()


def _softmax_items(k_blk, vt_blk, q_heads, kv_heads, m_ref, acc_ref, vt_rows, visible):
    items = []
    for hd in range(len(q_heads)):
        st = {}
        kv = kv_heads[hd]

        def scores(hd=hd, st=st, kv=kv):
            s = _dot(k_blk[:, kv * LANES:(kv + 1) * LANES], q_heads[hd])
            st["s"] = s if visible is None else jnp.where(visible, s, MASK_VALUE)

        def probs(hd=hd, st=st):
            s = st.pop("s")
            m_old = m_ref[hd]
            m_new = jnp.maximum(m_old, jnp.max(s, axis=0, keepdims=True))
            st["alpha"] = jnp.exp2(m_old - m_new)
            st["p"] = jnp.exp2(s - m_new).astype(BF16)
            m_ref[hd] = m_new

        def accumulate(hd=hd, st=st, kv=kv):
            pv = _dot(vt_blk[kv * vt_rows:(kv + 1) * vt_rows], st.pop("p"))
            acc_ref[hd] = st.pop("alpha") * acc_ref[hd] + pv

        items.append([scores, probs, accumulate])
    return items


def _shifted_softmax_items(k_blk, vt_blk, q_aug, kv_heads, acc_ref, vt_rows, ones_col, visible):
    items = []
    for hd in range(len(q_aug)):
        st = {}
        kv = kv_heads[hd]

        def scores(hd=hd, st=st, kv=kv):
            k_aug = jnp.concatenate([k_blk[:, kv * LANES:(kv + 1) * LANES], ones_col], axis=1)
            s = _dot(k_aug, q_aug[hd])
            st["s"] = s if visible is None else jnp.where(visible, s, MASK_VALUE)

        def probs(st=st):
            st["p"] = jnp.exp2(st.pop("s")).astype(BF16)

        def accumulate(hd=hd, st=st, kv=kv):
            acc_ref[hd] += _dot(vt_blk[kv * vt_rows:(kv + 1) * vt_rows], st.pop("p"))

        items.append([scores, probs, accumulate])
    return items


def _stick_breaking_items(k_blk, vt_blk, q_heads, c_ref, acc_ref, upper, valid):
    half = MXU_DIM
    sign_bit = jnp.uint32(0x80000000)
    items = []
    for hd in range(len(q_heads)):
        st = {}

        def logits(hd=hd, st=st):
            st["y"] = _dot(k_blk[:, (hd // 2) * LANES:(hd // 2 + 1) * LANES], q_heads[hd])

        def log_terms(hd=hd, st=st):
            y = st.pop("y")
            neg_abs = pltpu.bitcast(pltpu.bitcast(y, jnp.uint32) | sign_bit, F32)
            sp = jnp.log2(1.0 + jnp.exp2(neg_abs))
            log_beta = jnp.minimum(y, 0.0) - sp
            log_keep = log_beta - y
            if valid is not None:
                log_keep = jnp.where(valid, log_keep, 0.0)
            st.update(log_beta=log_beta, keep=log_keep.astype(BF16),
                      first=log_keep[0:1], mid=log_keep[half:half + 1])

        def suffix_sums(hd=hd, st=st):
            keep = st.pop("keep")
            st["later"] = _dot(upper, keep[half:])
            st["earlier"] = _dot(upper, keep[:half])

        def weights(hd=hd, st=st):
            later, earlier = st.pop("later"), st.pop("earlier")
            c_old = c_ref[hd]
            later_total = later[0:1] + st.pop("mid") + c_old
            between = jnp.concatenate([earlier + later_total, later + c_old], axis=0)
            a = jnp.exp2(st.pop("log_beta") + between)
            if valid is not None:
                a = jnp.where(valid, a, 0.0)
            c_ref[hd] = (earlier[0:1] + st.pop("first")) + later_total
            st["a"] = a.astype(BF16)

        def accumulate(hd=hd, st=st):
            acc_ref[hd] += _dot(vt_blk[hd * SB_DIM:(hd + 1) * SB_DIM], st.pop("a"))

        items.append([logits, log_terms, suffix_sums, weights, accumulate])
    return items


def _mixer_attn_kernel(lq1_ref, lk1_ref, lq2_ref, lk2_ref, subln_ref,
                       qta_ref, ka_ref, vta_ref, qtb_ref, kb_ref, vtb_ref, qtc_ref, kc_ref, vtc_ref,
                       *refs, lambda_init, shifted, nq):
    i = pl.program_id(1)
    ta = ATT_TILE
    if shifted:
        o_ref, worst_ref, c_ref, acc_a_ref, acc_b_ref, acc_c_ref, kmax_b_ref, kmax_c_ref = refs
    else:
        o_ref, c_ref, acc_a_ref, m_b_ref, acc_b_ref, m_c_ref, acc_c_ref = refs
        m_b_ref[...] = jnp.full_like(m_b_ref, MASK_VALUE)
        m_c_ref[...] = jnp.full_like(m_c_ref, MASK_VALUE)
    c_ref[...] = jnp.zeros_like(c_ref)
    acc_a_ref[...] = jnp.zeros_like(acc_a_ref)
    acc_b_ref[...] = jnp.zeros_like(acc_b_ref)
    acc_c_ref[...] = jnp.zeros_like(acc_c_ref)

    qta, qtb, qtc = qta_ref[0], qtb_ref[0], qtc_ref[0]
    q_a, q_b, q_c = [], [], []
    for hd in range(SB_HEADS):
        lo = (hd % 2) * SB_DIM
        q_a.append(qta[(hd // 2) * LANES:(hd // 2 + 1) * LANES] * _row_mask(LANES, lo, lo + SB_DIM, ta))
    for hd in range(DIFF_HEADS):
        qh = qtb[hd * LANES:(hd + 1) * LANES]
        q_b += [qh * _row_mask(LANES, 0, DIFF_QK, ta), qh * _row_mask(LANES, DIFF_QK, LANES, ta)]
    for hd in range(MLA_HEADS):
        q_c.append(qtc[hd * LANES:(hd + 1) * LANES])

    if shifted:
        @pl.when(i == 0)
        def _():
            for k_ref, kmax_ref in ((kb_ref, kmax_b_ref), (kc_ref, kmax_c_ref)):
                kmax = jnp.zeros((1, k_ref.shape[-1]), F32)
                for jb in range(nq):
                    kmax = jnp.maximum(kmax, jnp.max(jnp.abs(k_ref[0, jb].astype(F32)), axis=0, keepdims=True))
                kmax_ref[...] = kmax

        ones_col = jnp.where(lax.broadcasted_iota(jnp.int32, (ta, LANES), 1) == 0, 1.0, 0.0).astype(BF16)
        first_row = lax.broadcasted_iota(jnp.int32, (LANES, ta), 0) == 0
        first_of_16 = lax.broadcasted_iota(jnp.int32, (16, LANES), 0) == 0
        worst = jnp.zeros((1, 1), F32)

        def augment(q, kmax_row):
            kmax_tile = jnp.where(first_of_16, kmax_row, 0.0).astype(BF16)
            bound = _dot(kmax_tile, jnp.abs(q))[0:1]
            shift = bound * SHIFT_MARGIN
            extra = jnp.where(first_row, SHIFT_OFFSET - shift, 0.0).astype(BF16)
            return jnp.concatenate([q, extra], axis=0), jnp.max(shift, axis=1, keepdims=True)

        qb_aug, qc_aug = [], []
        for n, q in enumerate(q_b):
            kv = n // 2
            q_aug, top = augment(q, kmax_b_ref[:, kv * LANES:(kv + 1) * LANES])
            qb_aug.append(q_aug)
            worst = jnp.maximum(worst, top)
        for n, q in enumerate(q_c):
            q_aug, top = augment(q, kmax_c_ref[:, n * LANES:(n + 1) * LANES])
            qc_aug.append(q_aug)
            worst = jnp.maximum(worst, top)
        worst_ref[...] = jnp.broadcast_to(worst, worst_ref.shape)
    kk = lax.broadcasted_iota(jnp.int32, (ta, ta), 0)
    qq = lax.broadcasted_iota(jnp.int32, (ta, ta), 1)
    visible = kk <= qq
    valid_a = kk < qq
    upper = jnp.where(lax.broadcasted_iota(jnp.int32, (MXU_DIM, MXU_DIM), 1)
                      > lax.broadcasted_iota(jnp.int32, (MXU_DIM, MXU_DIM), 0), 1.0, 0.0).astype(BF16)

    def block(j, diag):
        a_items = _stick_breaking_items(ka_ref[0, j], vta_ref[0, j], q_a, c_ref, acc_a_ref, upper,
                                        valid_a if diag else None)
        kv_b, kv_c = [n // 2 for n in range(2 * DIFF_HEADS)], list(range(MLA_HEADS))
        mask = visible if diag else None
        if shifted:
            b_items = _shifted_softmax_items(kb_ref[0, j], vtb_ref[0, j], qb_aug, kv_b, acc_b_ref, DIFF_VT,
                                             ones_col, mask)
            c_items = _shifted_softmax_items(kc_ref[0, j], vtc_ref[0, j], qc_aug, kv_c, acc_c_ref, MLA_VT,
                                             ones_col, mask)
        else:
            b_items = _softmax_items(kb_ref[0, j], vtb_ref[0, j], q_b, kv_b, m_b_ref, acc_b_ref, DIFF_VT, mask)
            c_items = _softmax_items(kc_ref[0, j], vtc_ref[0, j], q_c, kv_c, m_c_ref, acc_c_ref, MLA_VT, mask)
        items = []
        for hd in range(SB_HEADS):
            items += [a_items[hd], b_items[2 * hd], c_items[hd], b_items[2 * hd + 1]]
        _emit_skewed(items)

    block(i, True)

    def body(jj, carry):
        block(i - 1 - jj, False)
        return carry

    lax.fori_loop(0, i, body, 0)

    o_ref[0, :, 0:256] = acc_a_ref[...].reshape(SB_HEADS * SB_DIM, ta).T.astype(o_ref.dtype)

    lam = (jnp.exp(jnp.sum(lq1_ref[...] * lk1_ref[...], axis=-1, keepdims=True))
           - jnp.exp(jnp.sum(lq2_ref[...] * lk2_ref[...], axis=-1, keepdims=True)) + lambda_init)
    outs = []
    for hd in range(DIFF_HEADS):
        acc1, acc2 = acc_b_ref[2 * hd], acc_b_ref[2 * hd + 1]
        o = (acc1[0:DIFF_V] / acc1[DIFF_V:DIFF_V + 1]
             - lam * (acc2[0:DIFF_V] / acc2[DIFF_V:DIFF_V + 1]))
        ms = jnp.mean(o * o, axis=0, keepdims=True)
        outs.append(o * lax.rsqrt(ms + EPS))
    o_nat = jnp.concatenate(outs, axis=0).T
    o_ref[0, :, 256:768] = (o_nat * subln_ref[...] * (1.0 - lambda_init)).astype(o_ref.dtype)

    outs = []
    for hd in range(MLA_HEADS):
        acc = acc_c_ref[hd]
        outs.append(acc[0:MLA_V] / acc[MLA_V:MLA_V + 1])
    o_ref[0, :, 768:1024] = jnp.concatenate(outs, axis=0).T.astype(o_ref.dtype)


def _mixer_attn(lq1, lk1, lq2, lk2, subln4, qta, ka, vta, qtb, kb, vtb, qtc, kc, vtc, layer, batch, seq):
    nq = seq // ATT_TILE
    lambda_init = 0.8 - 0.6 * math.exp(-0.3 * layer)
    vec = lambda n: pl.BlockSpec((None, 1, n), lambda b, i: (layer, 0, 0))
    q_t = lambda n: pl.BlockSpec((1, n, ATT_TILE), lambda b, i: (b, 0, i))
    per_batch = lambda r, c: pl.BlockSpec((1, nq, r, c), lambda b, i: (b, 0, 0, 0),
                                          pipeline_mode=pl.Buffered(1))
    operands = (lq1, lk1, lq2, lk2, subln4, qta, ka, vta, qtb, kb, vtb, qtc, kc, vtc)
    in_specs = [
        vec(DIFF_QK), vec(DIFF_QK), vec(DIFF_QK), vec(DIFF_QK), vec(DIFF_HEADS * DIFF_V),
        q_t(256), per_batch(ATT_TILE, 256), per_batch(256, ATT_TILE),
        q_t(512), per_batch(ATT_TILE, 512), per_batch(DIFF_HEADS * DIFF_VT, ATT_TILE),
        q_t(512), per_batch(ATT_TILE, 512), per_batch(MLA_HEADS * MLA_VT, ATT_TILE),
    ]
    mixed_spec = pl.BlockSpec((1, ATT_TILE, D_MODEL), lambda b, i: (b, i, 0))
    mixed_shape = jax.ShapeDtypeStruct((batch, seq, D_MODEL), BF16)
    state = lambda n, rows: pltpu.VMEM((n, rows, ATT_TILE), F32)

    def call(shifted):
        if shifted:
            out_specs = [mixed_spec, pl.BlockSpec((1, 1, 8, LANES), lambda b, i: (b, i, 0, 0))]
            out_shape = [mixed_shape, jax.ShapeDtypeStruct((batch, nq, 8, LANES), F32)]
            scratch = [state(SB_HEADS, 1), state(SB_HEADS, SB_DIM), state(2 * DIFF_HEADS, DIFF_VT),
                       state(MLA_HEADS, MLA_VT), pltpu.VMEM((1, 512), F32), pltpu.VMEM((1, 512), F32)]
        else:
            out_specs, out_shape = mixed_spec, mixed_shape
            scratch = [state(SB_HEADS, 1), state(SB_HEADS, SB_DIM), state(2 * DIFF_HEADS, 1),
                       state(2 * DIFF_HEADS, DIFF_VT), state(MLA_HEADS, 1), state(MLA_HEADS, MLA_VT)]
        return pl.pallas_call(
            functools.partial(_mixer_attn_kernel, lambda_init=lambda_init, shifted=shifted, nq=nq),
            grid=(batch, nq),
            in_specs=in_specs,
            out_specs=out_specs,
            out_shape=out_shape,
            scratch_shapes=scratch,
            compiler_params=_params(("arbitrary", "arbitrary")),
            name="mixer_attn" if shifted else "mixer_attn_online",
        )(*operands)

    mixed, worst = call(True)
    return lax.cond(jnp.max(worst) > SHIFT_LIMIT, lambda: call(False), lambda: mixed)


def _post_mixer_kernel(h_ref, mix_ref, p_ref, wo_ref, g2_ref, wg_ref, wv_ref, wd_ref,
                       gp_ref, wgate_ref, wpp_ref, fin_ref, o_ref, *, final):
    h = h_ref[...] + _dot(mix_ref[...], wo_ref[...])
    h = _half_step_ffn(h, g2_ref[...], wg_ref, wv_ref, wd_ref)
    gate = jax.nn.sigmoid(_dot(_rms_norm(h, gp_ref[...]).astype(BF16), wgate_ref[...]))
    out = h + _dot(p_ref[0].astype(BF16), wpp_ref[...]) * gate
    if final:
        out = _rms_norm(out, fin_ref[...])
    o_ref[...] = out


def _post_mixer(h, mixed, p, w_out, g2, w_gu, w_down, g_ple, w_gate, w_pproj, norm_final, layer, final):
    t = h.shape[0]
    resident = pl.Buffered(1)
    whole = lambda shape: pl.BlockSpec((None,) + shape, lambda i: (layer,) + (0,) * len(shape),
                                       pipeline_mode=resident)
    rows = pl.BlockSpec((ROW_TILE, D_MODEL), lambda i: (i, 0))
    return pl.pallas_call(
        functools.partial(_post_mixer_kernel, final=final),
        grid=(t // ROW_TILE,),
        in_specs=[
            rows, rows,
            pl.BlockSpec((1, ROW_TILE, PLE_DIM), lambda i: (layer, i, 0)),
            whole((D_MODEL, D_MODEL)),
            whole((1, D_MODEL)),
            pl.BlockSpec((None, D_MODEL, D_FF), lambda i: (layer, 0, 0), pipeline_mode=resident),
            pl.BlockSpec((None, D_MODEL, D_FF), lambda i: (layer, 0, 1), pipeline_mode=resident),
            whole((D_FF, D_MODEL)),
            whole((1, D_MODEL)),
            whole((D_MODEL, D_MODEL)),
            whole((PLE_DIM, D_MODEL)),
            pl.BlockSpec((1, D_MODEL), lambda i: (0, 0)),
        ],
        out_specs=rows,
        out_shape=jax.ShapeDtypeStruct((t, D_MODEL), F32),
        compiler_params=_params(("parallel",)),
        name="post_mixer",
    )(h, mixed, p, w_out, g2, w_gu, w_gu, w_down, g_ple, w_gate, w_pproj, norm_final)


def _rope_tables(positions):
    b, s = positions.shape
    pos = positions.astype(F32)[..., None]

    def cs(rot_dim):
        inv_freq = 1.0 / (ROPE_THETA ** (jnp.arange(0, rot_dim, 2, dtype=F32) / rot_dim))
        ang = pos * inv_freq
        return jnp.cos(ang), jnp.sin(ang)

    cos_d, sin_d = cs(2 * DIFF_ROT_HALF)
    cos_m, sin_m = cs(2 * MLA_ROT_HALF)
    one = lambda n: jnp.ones((b, s, n), F32)
    zero = lambda n: jnp.zeros((b, s, n), F32)
    c_d = jnp.tile(jnp.concatenate([cos_d, cos_d, one(48)], -1), (1, 1, 2))
    sp_d = jnp.tile(jnp.concatenate([zero(8), sin_d, zero(48)], -1), (1, 1, 2))
    sm_d = jnp.tile(jnp.concatenate([-sin_d, zero(56)], -1), (1, 1, 2))
    c_m = jnp.concatenate([one(64), cos_m, cos_m, one(32)], -1)
    sp_m = jnp.concatenate([zero(80), sin_m, zero(32)], -1)
    sm_m = jnp.concatenate([zero(64), -sin_m, zero(48)], -1)
    tabn = jnp.concatenate([c_d, sp_d, sm_d, c_m, sp_m, sm_m], -1).reshape(b * s, 6 * LANES)
    tabt = jnp.swapaxes(jnp.concatenate([cos_d, sin_d, cos_m, sin_m], -1), 1, 2)
    return tabn, tabt


def _prep_mixer_weights(w_in, mla_w_uq, mla_w_ukv):
    n_l = w_in.shape[0]
    seg = lambda i: w_in[:, :, _OFFS[i]:_OFFS[i + 1]]
    a_q, a_k, a_v, b_q, b_k, b_v, c_q, c_kv, k_pe = (seg(i) for i in range(9))
    zeros = lambda n: jnp.zeros((n_l, D_MODEL, n), w_in.dtype)
    k_pe_pad = jnp.concatenate([zeros(MLA_NOPE), k_pe, zeros(LANES - MLA_NOPE - MLA_ROPE)], -1)
    wnat = jnp.concatenate([a_k, b_k, c_q, c_kv, k_pe_pad], -1).astype(BF16)
    wtr = jnp.swapaxes(jnp.concatenate([a_q, a_v, b_q, b_v], -1), 1, 2).astype(BF16)
    uq = mla_w_uq.reshape(n_l, MLA_Q_RANK, MLA_HEADS, MLA_NOPE + MLA_ROPE)
    uq = jnp.pad(uq, ((0, 0), (0, 0), (0, 0), (0, LANES - MLA_NOPE - MLA_ROPE)))
    wuq = jnp.swapaxes(uq.reshape(n_l, MLA_Q_RANK, MLA_HEADS * LANES), 1, 2).astype(BF16)
    ukv = mla_w_ukv.reshape(n_l, MLA_KV_RANK, MLA_HEADS, MLA_NOPE + MLA_V)
    uk = jnp.pad(ukv[..., :MLA_NOPE], ((0, 0), (0, 0), (0, 0), (0, LANES - MLA_NOPE)))
    wuk = uk.reshape(n_l, MLA_KV_RANK, MLA_HEADS * LANES).astype(BF16)
    wuv = jnp.swapaxes(ukv[..., MLA_NOPE:].reshape(n_l, MLA_KV_RANK, MLA_HEADS * MLA_V), 1, 2).astype(BF16)
    return wnat, wtr, wuq, wuk, wuv


def kernel(x, p, positions, norm_ffn1, w_ffn1_gu, w_ffn1_down, norm_mix, w_in, mla_q_norm, mla_w_uq,
           mla_kv_norm, mla_w_ukv, diff_lambda_q1, diff_lambda_k1, diff_lambda_q2, diff_lambda_k2,
           diff_subln, w_out, norm_ffn2, w_ffn2_gu, w_ffn2_down, norm_ple, w_ple_gate, w_ple_proj,
           norm_final):
    batch, seq, _ = x.shape
    depth = w_in.shape[0]
    assert seq % ATT_TILE == 0 and ATT_TILE == ROW_TILE
    t = batch * seq
    row3 = lambda a: a.reshape(a.shape[0], 1, a.shape[1])

    tabn, tabt = _rope_tables(positions)
    wnat, wtr, wuq, wuk, wuv = _prep_mixer_weights(w_in, mla_w_uq, mla_w_ukv)
    w1_gu, w1_d = w_ffn1_gu.astype(BF16), w_ffn1_down.astype(BF16)
    w2_gu, w2_d = w_ffn2_gu.astype(BF16), w_ffn2_down.astype(BF16)
    w_out_b = w_out.astype(BF16)
    w_gate_b, w_pproj_b = w_ple_gate.astype(BF16), w_ple_proj.astype(BF16)
    g_ffn1, g_mix, g_ffn2, g_ple = row3(norm_ffn1), row3(norm_mix), row3(norm_ffn2), row3(norm_ple)
    g_q, g_kv = row3(mla_q_norm), row3(mla_kv_norm)
    lq1, lk1, lq2, lk2 = (row3(a) for a in (diff_lambda_q1, diff_lambda_k1, diff_lambda_q2, diff_lambda_k2))
    subln4 = row3(jnp.tile(diff_subln, (1, DIFF_HEADS)))
    p_rows = p.reshape(depth, t, PLE_DIM)
    fin = norm_final.reshape(1, D_MODEL)

    h = x.reshape(t, D_MODEL)
    for layer in range(depth):
        h, qta, ka, vta, qtb, kb, vtb, qtc, kc, vtc = _pre_mixer(
            h, g_ffn1, w1_gu, w1_d, g_mix, wnat, wtr, g_q, g_kv, wuq, wuk, wuv, tabn, tabt, layer, batch, seq)
        mixed = _mixer_attn(lq1, lk1, lq2, lk2, subln4, qta, ka, vta, qtb, kb, vtb, qtc, kc, vtc,
                            layer, batch, seq)
        h = _post_mixer(h, mixed.reshape(t, D_MODEL), p_rows, w_out_b, g_ffn2, w2_gu, w2_d, g_ple,
                        w_gate_b, w_pproj_b, fin, layer, final=(layer == depth - 1))
    return h.reshape(batch, seq, D_MODEL)
```

```python
import functools
import math

import jax
import jax.numpy as jnp
import numpy as np
from jax import lax
from jax.experimental import pallas as pl
from jax.experimental.pallas import tpu as pltpu

F32 = jnp.float32
BF16 = jnp.bfloat16

D_MODEL = 1024
SB_HEADS, SB_DIM = 4, 64
DIFF_HEADS, DIFF_QK, DIFF_V = 4, 64, 128
MLA_HEADS, MLA_NOPE, MLA_ROPE, MLA_V = 4, 64, 32, 64
MLA_Q_RANK, MLA_KV_RANK = 256, 128
D_FF = 2816
PLE_DIM = 256
ROPE_THETA = 500000.0
DIFF_ROT_HALF = 8
MLA_ROT_HALF = 16
EPS = 1e-6
LOG2E = 1.4426950408889634
MASK_VALUE = -1e30
SHIFT_OFFSET = 60.0
SHIFT_LIMIT = 90.0
SHIFT_MARGIN = 1.0 + 2.0 ** -6

LANES = 128
MXU_DIM = 256
VMEM_LIMIT_BYTES = 56 * 1024 * 1024

ROW_TILE = 512
FF_CHUNK = 2 * MXU_DIM
ATT_TILE = 2 * MXU_DIM
V_PAD = 16
DIFF_VT = DIFF_V + V_PAD
MLA_VT = MLA_V + V_PAD

_SPLITS = (256, 256, 256, 512, 512, 512, MLA_Q_RANK, MLA_KV_RANK, MLA_ROPE)
_OFFS = tuple(int(sum(_SPLITS[:i])) for i in range(len(_SPLITS) + 1))
N_NAT = 256 + 512 + MLA_Q_RANK + MLA_KV_RANK + LANES
N_TR = 256 + 256 + 512 + 512


def _params(sem):
    return pltpu.CompilerParams(dimension_semantics=sem, vmem_limit_bytes=VMEM_LIMIT_BYTES)


def _rms_norm(x, gain):
    ms = jnp.mean(x * x, axis=-1, keepdims=True)
    return x * lax.rsqrt(ms + EPS) * gain


def _dot(a, b):
    return jnp.dot(a, b, preferred_element_type=F32)


def _dot_nt(a, b):
    return lax.dot_general(a, b, (((1,), (1,)), ((), ())), preferred_element_type=F32)


def _half_step_ffn(h, gain, wg_ref, wv_ref, wd_ref):
    xn = _rms_norm(h, gain).astype(BF16)
    acc = None
    for lo in range(0, D_FF, FF_CHUNK):
        hi = min(lo + FF_CHUNK, D_FF)
        g = _dot(xn, wg_ref[:, lo:hi])
        v = _dot(xn, wv_ref[:, lo:hi])
        a = (g * jax.nn.sigmoid(g) * v).astype(BF16)
        part = _dot(a, wd_ref[lo:hi, :])
        acc = part if acc is None else acc + part
    return h + 0.5 * acc


def _rope_lanes(x, c, sp, sm, shift):
    return x * c + pltpu.roll(x, shift, 1) * sp + pltpu.roll(x, LANES - shift, 1) * sm


def _rope_rows(x, base, half, cos, sin):
    x1 = x[base:base + half]
    x2 = x[base + half:base + 2 * half]
    return x1 * cos - x2 * sin, x2 * cos + x1 * sin


def _pre_mixer_kernel(h_ref, g1_ref, wg_ref, wv_ref, wd_ref,
                      gain_ref, wnat_ref, wtr_ref, qn_ref, kvn_ref, wuq_ref, wuk_ref, wuv_ref,
                      tabn_ref, tabt_ref,
                      h_out_ref, qta_ref, ka_ref, vta_ref, qtb_ref, kb_ref, vtb_ref, qtc_ref, kc_ref, vtc_ref):
    h = _half_step_ffn(h_ref[...], g1_ref[...], wg_ref, wv_ref, wd_ref)
    h_out_ref[...] = h

    u = _rms_norm(h, gain_ref[...]).astype(BF16)
    nat = _dot(u, wnat_ref[...])
    tr = _dot_nt(wtr_ref[...], u)

    tabn = tabn_ref[...]
    tabt = tabt_ref[0]
    cos_d, sin_d = tabt[0:8], tabt[8:16]
    cos_m, sin_m = tabt[16:32], tabt[32:48]

    tm = h_ref.shape[0]
    ones_row = jnp.where(lax.broadcasted_iota(jnp.int32, (V_PAD, tm), 0) == 0, 1.0, 0.0)

    def with_ones(vt, n_heads, dv):
        parts = []
        for hd in range(n_heads):
            parts += [vt[hd * dv:(hd + 1) * dv], ones_row]
        return jnp.concatenate(parts, axis=0)

    qta_ref[0] = (tr[0:256] * (SB_DIM ** -0.5 * LOG2E)).astype(BF16)
    ka_ref[0, 0] = nat[:, 0:256].astype(BF16)
    vta_ref[0, 0] = tr[256:512].astype(BF16)

    qb = tr[512:1024]
    parts = []
    for ch in range(2 * DIFF_HEADS):
        base = ch * DIFF_QK
        r1, r2 = _rope_rows(qb, base, DIFF_ROT_HALF, cos_d, sin_d)
        parts += [r1, r2, qb[base + 2 * DIFF_ROT_HALF:base + DIFF_QK]]
    qtb_ref[0] = (jnp.concatenate(parts, axis=0) * (DIFF_QK ** -0.5 * LOG2E)).astype(BF16)
    kb = nat[:, 256:768]
    c_d, sp_d, sm_d = tabn[:, 0:128], tabn[:, 128:256], tabn[:, 256:384]
    kb = jnp.concatenate(
        [_rope_lanes(kb[:, c * LANES:(c + 1) * LANES], c_d, sp_d, sm_d, DIFF_ROT_HALF) for c in range(4)], axis=1)
    kb_ref[0, 0] = kb.astype(BF16)
    vtb_ref[0, 0] = with_ones(tr[1024:1536], DIFF_HEADS, DIFF_V).astype(BF16)

    cq = _rms_norm(nat[:, 768:1024], qn_ref[...]).astype(BF16)
    qc = _dot_nt(wuq_ref[...], cq)
    parts = []
    for hd in range(MLA_HEADS):
        base = hd * LANES
        r1, r2 = _rope_rows(qc, base + MLA_NOPE, MLA_ROT_HALF, cos_m, sin_m)
        parts += [qc[base:base + MLA_NOPE], r1, r2, qc[base + MLA_NOPE + MLA_ROPE:base + LANES]]
    qtc_ref[0] = (jnp.concatenate(parts, axis=0) * ((MLA_NOPE + MLA_ROPE) ** -0.5 * LOG2E)).astype(BF16)
    ckv = _rms_norm(nat[:, 1024:1152], kvn_ref[...]).astype(BF16)
    k_nope = _dot(ckv, wuk_ref[...])
    c_m, sp_m, sm_m = tabn[:, 384:512], tabn[:, 512:640], tabn[:, 640:768]
    k_rot = _rope_lanes(nat[:, 1152:1280], c_m, sp_m, sm_m, MLA_ROT_HALF)
    kc = jnp.concatenate([k_nope[:, hd * LANES:(hd + 1) * LANES] + k_rot for hd in range(MLA_HEADS)], axis=1)
    kc_ref[0, 0] = kc.astype(BF16)
    vtc_ref[0, 0] = with_ones(_dot_nt(wuv_ref[...], ckv), MLA_HEADS, MLA_V).astype(BF16)


def _pre_mixer(h, g1, w_gu, w_down, gain, wnat, wtr, qn, kvn, wuq, wuk, wuv, tabn, tabt, layer, batch, seq):
    t = h.shape[0]
    nblk = seq // ATT_TILE
    resident = pl.Buffered(1)

    def row_blocked(n):
        return (jax.ShapeDtypeStruct((batch, nblk, ATT_TILE, n), BF16),
                pl.BlockSpec((1, 1, ATT_TILE, n), lambda b, i: (b, i, 0, 0)))

    def col_blocked(n):
        return (jax.ShapeDtypeStruct((batch, nblk, n, ATT_TILE), BF16),
                pl.BlockSpec((1, 1, n, ATT_TILE), lambda b, i: (b, i, 0, 0)))

    def q_t(n):
        return (jax.ShapeDtypeStruct((batch, n, seq), BF16),
                pl.BlockSpec((1, n, ATT_TILE), lambda b, i: (b, 0, i)))

    rows = pl.BlockSpec((ATT_TILE, D_MODEL), lambda b, i: (b * nblk + i, 0))
    outs = [(jax.ShapeDtypeStruct((t, D_MODEL), F32), rows),
            q_t(256), row_blocked(256), col_blocked(256),
            q_t(512), row_blocked(512), col_blocked(DIFF_HEADS * DIFF_VT),
            q_t(512), row_blocked(512), col_blocked(MLA_HEADS * MLA_VT)]
    whole = lambda shape: pl.BlockSpec((None,) + shape, lambda b, i: (layer,) + (0,) * len(shape),
                                       pipeline_mode=resident)
    return pl.pallas_call(
        _pre_mixer_kernel,
        grid=(batch, nblk),
        in_specs=[
            rows,
            whole((1, D_MODEL)),
            pl.BlockSpec((None, D_MODEL, D_FF), lambda b, i: (layer, 0, 0), pipeline_mode=resident),
            pl.BlockSpec((None, D_MODEL, D_FF), lambda b, i: (layer, 0, 1), pipeline_mode=resident),
            whole((D_FF, D_MODEL)),
            whole((1, D_MODEL)),
            whole((D_MODEL, N_NAT)),
            whole((N_TR, D_MODEL)),
            whole((1, MLA_Q_RANK)),
            whole((1, MLA_KV_RANK)),
            whole((4 * LANES, MLA_Q_RANK)),
            whole((MLA_KV_RANK, 4 * LANES)),
            whole((MLA_HEADS * MLA_V, MLA_KV_RANK)),
            pl.BlockSpec((ATT_TILE, 6 * LANES), lambda b, i: (b * nblk + i, 0)),
            pl.BlockSpec((1, 48, ATT_TILE), lambda b, i: (b, 0, i)),
        ],
        out_specs=[o[1] for o in outs],
        out_shape=[o[0] for o in outs],
        compiler_params=_params(("parallel", "parallel")),
        name="pre_mixer",
    )(h, g1, w_gu, w_gu, w_down, gain, wnat, wtr, qn, kvn, wuq, wuk, wuv, tabn, tabt)


def _row_mask(n_rows, lo, hi, n_cols):
    r = lax.broadcasted_iota(jnp.int32, (n_rows, n_cols), 0)
    return jnp.where((r >= lo) & (r < hi), 1.0, 0.0).astype(BF16)


def _emit_skewed(items):
    n_steps = max(k + len(stages) for k, stages in enumerate(items))
    for step in range(n_steps):
        for k, stages in enumerate(items):
            s_idx = step - k
            if 0 <= s_idx < len(stages):
                stages[s_idx]()


def _softmax_items(k_blk, vt_blk, q_heads, kv_heads, m_ref, acc_ref, vt_rows, visible):
    items = []
    for hd in range(len(q_heads)):
        st = {}
        kv = kv_heads[hd]

        def scores(hd=hd, st=st, kv=kv):
            s = _dot(k_blk[:, kv * LANES:(kv + 1) * LANES], q_heads[hd])
            st["s"] = s if visible is None else jnp.where(visible, s, MASK_VALUE)

        def probs(hd=hd, st=st):
            s = st.pop("s")
            m_old = m_ref[hd]
            m_new = jnp.maximum(m_old, jnp.max(s, axis=0, keepdims=True))
            st["alpha"] = jnp.exp2(m_old - m_new)
            st["p"] = jnp.exp2(s - m_new).astype(BF16)
            m_ref[hd] = m_new

        def accumulate(hd=hd, st=st, kv=kv):
            pv = _dot(vt_blk[kv * vt_rows:(kv + 1) * vt_rows], st.pop("p"))
            acc_ref[hd] = st.pop("alpha") * acc_ref[hd] + pv

        items.append([scores, probs, accumulate])
    return items


def _shifted_softmax_items(k_blk, vt_blk, q_aug, kv_heads, acc_ref, vt_rows, ones_col, diag_masks):
    half = MXU_DIM
    items = []
    for hd in range(len(q_aug)):
        st = {}
        kv = kv_heads[hd]

        def scores(hd=hd, st=st, kv=kv):
            k_aug = jnp.concatenate([k_blk[:, kv * LANES:(kv + 1) * LANES], ones_col], axis=1)
            if diag_masks is None:
                st["s"] = (_dot(k_aug, q_aug[hd]),)
            else:
                wide, square = diag_masks
                st["s"] = (jnp.where(wide, _dot(k_aug[:half], q_aug[hd]), MASK_VALUE),
                           jnp.where(square, _dot(k_aug[half:], q_aug[hd][:, half:]), MASK_VALUE))

        def probs(st=st):
            st["p"] = tuple(jnp.exp2(s).astype(BF16) for s in st.pop("s"))

        def accumulate(hd=hd, st=st, kv=kv):
            vt = vt_blk[kv * vt_rows:(kv + 1) * vt_rows]
            p = st.pop("p")
            if diag_masks is None:
                acc_ref[hd] += _dot(vt, p[0])
            else:
                acc_ref[hd] += _dot(vt[:, :half], p[0])
                acc_ref[hd, :, half:] += _dot(vt[:, half:], p[1])

        items.append([scores, probs, accumulate])
    return items


def _log2_sigmoid_terms(y):
    neg_abs = pltpu.bitcast(pltpu.bitcast(y, jnp.uint32) | jnp.uint32(0x80000000), F32)
    log_beta = jnp.minimum(y, 0.0) - jnp.log2(1.0 + jnp.exp2(neg_abs))
    return log_beta, log_beta - y


def _stick_breaking_items(k_blk, vt_blk, q_heads, c_ref, acc_ref, upper):
    half = MXU_DIM
    items = []
    for hd in range(len(q_heads)):
        st = {}

        def logits(hd=hd, st=st):
            st["y"] = _dot(k_blk[:, (hd // 2) * LANES:(hd // 2 + 1) * LANES], q_heads[hd])

        def log_terms(st=st):
            log_beta, log_keep = _log2_sigmoid_terms(st.pop("y"))
            st.update(log_beta=log_beta, keep=log_keep.astype(BF16),
                      first=log_keep[0:1], mid=log_keep[half:half + 1])

        def suffix_sums(st=st):
            keep = st.pop("keep")
            st["later"] = _dot(upper, keep[half:])
            st["earlier"] = _dot(upper, keep[:half])

        def weights(hd=hd, st=st):
            later, earlier = st.pop("later"), st.pop("earlier")
            c_old = c_ref[hd]
            later_total = later[0:1] + st.pop("mid") + c_old
            between = jnp.concatenate([earlier + later_total, later + c_old], axis=0)
            st["a"] = jnp.exp2(st.pop("log_beta") + between).astype(BF16)
            c_ref[hd] = (earlier[0:1] + st.pop("first")) + later_total

        def accumulate(hd=hd, st=st):
            acc_ref[hd] += _dot(vt_blk[hd * SB_DIM:(hd + 1) * SB_DIM], st.pop("a"))

        items.append([logits, log_terms, suffix_sums, weights, accumulate])
    return items


def _stick_breaking_diag_items(k_blk, vt_blk, q_heads, c_ref, acc_ref, upper, masks):
    half = MXU_DIM
    items = []
    for hd in range(len(q_heads)):
        st = {}

        def logits(hd=hd, st=st):
            kh = k_blk[:, (hd // 2) * LANES:(hd // 2 + 1) * LANES]
            st["y"] = (_dot(kh[:half], q_heads[hd]), _dot(kh[half:], q_heads[hd][:, half:]))

        def log_terms(st=st):
            out = []
            for y, mask in zip(st.pop("y"), masks):
                log_beta, log_keep = _log2_sigmoid_terms(y)
                log_keep = jnp.where(mask, log_keep, 0.0)
                out.append((log_beta, log_keep.astype(BF16), log_keep[0:1]))
            st["terms"] = out

        def suffix_sums(st=st):
            st["sums"] = tuple(_dot(upper, keep) for _, keep, _ in st["terms"])

        def weights(hd=hd, st=st):
            earlier, later = st.pop("sums")
            (lb_e, _, first_e), (lb_l, _, first_l) = st.pop("terms")
            later_total = later[0:1] + first_l
            later_total = jnp.concatenate([jnp.zeros_like(later_total), later_total], axis=1)
            a_earlier = jnp.where(masks[0], jnp.exp2(lb_e + (earlier + later_total)), 0.0)
            a_later = jnp.where(masks[1], jnp.exp2(lb_l + later), 0.0)
            c_ref[hd] = (earlier[0:1] + first_e) + later_total
            st["a"] = (a_earlier.astype(BF16), a_later.astype(BF16))

        def accumulate(hd=hd, st=st):
            vt = vt_blk[hd * SB_DIM:(hd + 1) * SB_DIM]
            a_earlier, a_later = st.pop("a")
            acc_ref[hd] += _dot(vt[:, :half], a_earlier)
            acc_ref[hd, :, half:] += _dot(vt[:, half:], a_later)

        items.append([logits, log_terms, suffix_sums, weights, accumulate])
    return items


def _mixer_attn_kernel(lq1_ref, lk1_ref, lq2_ref, lk2_ref, subln_ref,
                       qta_ref, ka_ref, vta_ref, qtb_ref, kb_ref, vtb_ref, qtc_ref, kc_ref, vtc_ref,
                       *refs, lambda_init, shifted, nq):
    i = pl.program_id(1)
    ta = ATT_TILE
    if shifted:
        o_ref, worst_ref, c_ref, acc_a_ref, acc_b_ref, acc_c_ref, kmax_b_ref, kmax_c_ref = refs
    else:
        o_ref, c_ref, acc_a_ref, m_b_ref, acc_b_ref, m_c_ref, acc_c_ref = refs
        m_b_ref[...] = jnp.full_like(m_b_ref, MASK_VALUE)
        m_c_ref[...] = jnp.full_like(m_c_ref, MASK_VALUE)
    c_ref[...] = jnp.zeros_like(c_ref)
    acc_a_ref[...] = jnp.zeros_like(acc_a_ref)
    acc_b_ref[...] = jnp.zeros_like(acc_b_ref)
    acc_c_ref[...] = jnp.zeros_like(acc_c_ref)

    qta, qtb, qtc = qta_ref[0], qtb_ref[0], qtc_ref[0]
    q_a, q_b, q_c = [], [], []
    for hd in range(SB_HEADS):
        lo = (hd % 2) * SB_DIM
        q_a.append(qta[(hd // 2) * LANES:(hd // 2 + 1) * LANES] * _row_mask(LANES, lo, lo + SB_DIM, ta))
    for hd in range(DIFF_HEADS):
        qh = qtb[hd * LANES:(hd + 1) * LANES]
        q_b += [qh * _row_mask(LANES, 0, DIFF_QK, ta), qh * _row_mask(LANES, DIFF_QK, LANES, ta)]
    for hd in range(MLA_HEADS):
        q_c.append(qtc[hd * LANES:(hd + 1) * LANES])

    if shifted:
        @pl.when(i == 0)
        def _():
            for k_ref, kmax_ref in ((kb_ref, kmax_b_ref), (kc_ref, kmax_c_ref)):
                kmax = jnp.zeros((1, k_ref.shape[-1]), F32)
                for jb in range(nq):
                    kmax = jnp.maximum(kmax, jnp.max(jnp.abs(k_ref[0, jb].astype(F32)), axis=0, keepdims=True))
                kmax_ref[...] = kmax

        ones_col = jnp.where(lax.broadcasted_iota(jnp.int32, (ta, LANES), 1) == 0, 1.0, 0.0).astype(BF16)
        first_row = lax.broadcasted_iota(jnp.int32, (LANES, ta), 0) == 0
        first_of_16 = lax.broadcasted_iota(jnp.int32, (16, LANES), 0) == 0
        worst = jnp.zeros((1, 1), F32)

        def augment(q, kmax_row):
            kmax_tile = jnp.where(first_of_16, kmax_row, 0.0).astype(BF16)
            bound = _dot(kmax_tile, jnp.abs(q))[0:1]
            shift = bound * SHIFT_MARGIN
            extra = jnp.where(first_row, SHIFT_OFFSET - shift, 0.0).astype(BF16)
            return jnp.concatenate([q, extra], axis=0), jnp.max(shift, axis=1, keepdims=True)

        qb_aug, qc_aug = [], []
        for n, q in enumerate(q_b):
            kv = n // 2
            q_aug, top = augment(q, kmax_b_ref[:, kv * LANES:(kv + 1) * LANES])
            qb_aug.append(q_aug)
            worst = jnp.maximum(worst, top)
        for n, q in enumerate(q_c):
            q_aug, top = augment(q, kmax_c_ref[:, n * LANES:(n + 1) * LANES])
            qc_aug.append(q_aug)
            worst = jnp.maximum(worst, top)
        worst_ref[...] = jnp.broadcast_to(worst, worst_ref.shape)
    half = MXU_DIM
    kk = lax.broadcasted_iota(jnp.int32, (half, ta), 0)
    qq = lax.broadcasted_iota(jnp.int32, (half, ta), 1)
    kk_sq = lax.broadcasted_iota(jnp.int32, (half, half), 0)
    qq_sq = lax.broadcasted_iota(jnp.int32, (half, half), 1)
    visible = (kk <= qq, kk_sq <= qq_sq)
    strictly_before = (kk < qq, kk_sq < qq_sq)
    kk_full = lax.broadcasted_iota(jnp.int32, (ta, ta), 0)
    qq_full = lax.broadcasted_iota(jnp.int32, (ta, ta), 1)
    upper = jnp.where(qq_sq > kk_sq, 1.0, 0.0).astype(BF16)

    def block(j, diag):
        if diag:
            a_items = _stick_breaking_diag_items(ka_ref[0, j], vta_ref[0, j], q_a, c_ref, acc_a_ref, upper,
                                                 strictly_before)
        else:
            a_items = _stick_breaking_items(ka_ref[0, j], vta_ref[0, j], q_a, c_ref, acc_a_ref, upper)
        kv_b, kv_c = [n // 2 for n in range(2 * DIFF_HEADS)], list(range(MLA_HEADS))
        if shifted:
            masks = visible if diag else None
            b_items = _shifted_softmax_items(kb_ref[0, j], vtb_ref[0, j], qb_aug, kv_b, acc_b_ref, DIFF_VT,
                                             ones_col, masks)
            c_items = _shifted_softmax_items(kc_ref[0, j], vtc_ref[0, j], qc_aug, kv_c, acc_c_ref, MLA_VT,
                                             ones_col, masks)
        else:
            mask = (kk_full <= qq_full) if diag else None
            b_items = _softmax_items(kb_ref[0, j], vtb_ref[0, j], q_b, kv_b, m_b_ref, acc_b_ref, DIFF_VT, mask)
            c_items = _softmax_items(kc_ref[0, j], vtc_ref[0, j], q_c, kv_c, m_c_ref, acc_c_ref, MLA_VT, mask)
        items = []
        for hd in range(SB_HEADS):
            items += [a_items[hd], b_items[2 * hd], c_items[hd], b_items[2 * hd + 1]]
        _emit_skewed(items)

    block(i, True)

    def body(jj, carry):
        block(i - 1 - jj, False)
        return carry

    lax.fori_loop(0, i, body, 0)

    o_ref[0, :, 0:256] = acc_a_ref[...].reshape(SB_HEADS * SB_DIM, ta).T.astype(o_ref.dtype)

    lam = (jnp.exp(jnp.sum(lq1_ref[...] * lk1_ref[...], axis=-1, keepdims=True))
           - jnp.exp(jnp.sum(lq2_ref[...] * lk2_ref[...], axis=-1, keepdims=True)) + lambda_init)
    outs = []
    for hd in range(DIFF_HEADS):
        acc1, acc2 = acc_b_ref[2 * hd], acc_b_ref[2 * hd + 1]
        o = (acc1[0:DIFF_V] / acc1[DIFF_V:DIFF_V + 1]
             - lam * (acc2[0:DIFF_V] / acc2[DIFF_V:DIFF_V + 1]))
        ms = jnp.mean(o * o, axis=0, keepdims=True)
        outs.append(o * lax.rsqrt(ms + EPS))
    o_nat = jnp.concatenate(outs, axis=0).T
    o_ref[0, :, 256:768] = (o_nat * subln_ref[...] * (1.0 - lambda_init)).astype(o_ref.dtype)

    outs = []
    for hd in range(MLA_HEADS):
        acc = acc_c_ref[hd]
        outs.append(acc[0:MLA_V] / acc[MLA_V:MLA_V + 1])
    o_ref[0, :, 768:1024] = jnp.concatenate(outs, axis=0).T.astype(o_ref.dtype)


def _mixer_attn(lq1, lk1, lq2, lk2, subln4, qta, ka, vta, qtb, kb, vtb, qtc, kc, vtc, layer, batch, seq):
    nq = seq // ATT_TILE
    lambda_init = 0.8 - 0.6 * math.exp(-0.3 * layer)
    vec = lambda n: pl.BlockSpec((None, 1, n), lambda b, i: (layer, 0, 0))
    q_t = lambda n: pl.BlockSpec((1, n, ATT_TILE), lambda b, i: (b, 0, i))
    per_batch = lambda r, c: pl.BlockSpec((1, nq, r, c), lambda b, i: (b, 0, 0, 0),
                                          pipeline_mode=pl.Buffered(1))
    operands = (lq1, lk1, lq2, lk2, subln4, qta, ka, vta, qtb, kb, vtb, qtc, kc, vtc)
    in_specs = [
        vec(DIFF_QK), vec(DIFF_QK), vec(DIFF_QK), vec(DIFF_QK), vec(DIFF_HEADS * DIFF_V),
        q_t(256), per_batch(ATT_TILE, 256), per_batch(256, ATT_TILE),
        q_t(512), per_batch(ATT_TILE, 512), per_batch(DIFF_HEADS * DIFF_VT, ATT_TILE),
        q_t(512), per_batch(ATT_TILE, 512), per_batch(MLA_HEADS * MLA_VT, ATT_TILE),
    ]
    mixed_spec = pl.BlockSpec((1, ATT_TILE, D_MODEL), lambda b, i: (b, i, 0))
    mixed_shape = jax.ShapeDtypeStruct((batch, seq, D_MODEL), BF16)
    state = lambda n, rows: pltpu.VMEM((n, rows, ATT_TILE), F32)

    def call(shifted):
        if shifted:
            out_specs = [mixed_spec, pl.BlockSpec((1, 1, 8, LANES), lambda b, i: (b, i, 0, 0))]
            out_shape = [mixed_shape, jax.ShapeDtypeStruct((batch, nq, 8, LANES), F32)]
            scratch = [state(SB_HEADS, 1), state(SB_HEADS, SB_DIM), state(2 * DIFF_HEADS, DIFF_VT),
                       state(MLA_HEADS, MLA_VT), pltpu.VMEM((1, 512), F32), pltpu.VMEM((1, 512), F32)]
        else:
            out_specs, out_shape = mixed_spec, mixed_shape
            scratch = [state(SB_HEADS, 1), state(SB_HEADS, SB_DIM), state(2 * DIFF_HEADS, 1),
                       state(2 * DIFF_HEADS, DIFF_VT), state(MLA_HEADS, 1), state(MLA_HEADS, MLA_VT)]
        return pl.pallas_call(
            functools.partial(_mixer_attn_kernel, lambda_init=lambda_init, shifted=shifted, nq=nq),
            grid=(batch, nq),
            in_specs=in_specs,
            out_specs=out_specs,
            out_shape=out_shape,
            scratch_shapes=scratch,
            compiler_params=_params(("arbitrary", "arbitrary")),
            name="mixer_attn" if shifted else "mixer_attn_online",
        )(*operands)

    mixed, worst = call(True)
    return lax.cond(jnp.max(worst) > SHIFT_LIMIT, lambda: call(False), lambda: mixed)


def _post_mixer_kernel(h_ref, mix_ref, p_ref, wo_ref, g2_ref, wg_ref, wv_ref, wd_ref,
                       gp_ref, wgate_ref, wpp_ref, fin_ref, o_ref, *, final):
    h = h_ref[...] + _dot(mix_ref[...], wo_ref[...])
    h = _half_step_ffn(h, g2_ref[...], wg_ref, wv_ref, wd_ref)
    gate = jax.nn.sigmoid(_dot(_rms_norm(h, gp_ref[...]).astype(BF16), wgate_ref[...]))
    out = h + _dot(p_ref[0].astype(BF16), wpp_ref[...]) * gate
    if final:
        out = _rms_norm(out, fin_ref[...])
    o_ref[...] = out


def _post_mixer(h, mixed, p, w_out, g2, w_gu, w_down, g_ple, w_gate, w_pproj, norm_final, layer, final):
    t = h.shape[0]
    resident = pl.Buffered(1)
    whole = lambda shape: pl.BlockSpec((None,) + shape, lambda i: (layer,) + (0,) * len(shape),
                                       pipeline_mode=resident)
    rows = pl.BlockSpec((ROW_TILE, D_MODEL), lambda i: (i, 0))
    return pl.pallas_call(
        functools.partial(_post_mixer_kernel, final=final),
        grid=(t // ROW_TILE,),
        in_specs=[
            rows, rows,
            pl.BlockSpec((1, ROW_TILE, PLE_DIM), lambda i: (layer, i, 0)),
            whole((D_MODEL, D_MODEL)),
            whole((1, D_MODEL)),
            pl.BlockSpec((None, D_MODEL, D_FF), lambda i: (layer, 0, 0), pipeline_mode=resident),
            pl.BlockSpec((None, D_MODEL, D_FF), lambda i: (layer, 0, 1), pipeline_mode=resident),
            whole((D_FF, D_MODEL)),
            whole((1, D_MODEL)),
            whole((D_MODEL, D_MODEL)),
            whole((PLE_DIM, D_MODEL)),
            pl.BlockSpec((1, D_MODEL), lambda i: (0, 0)),
        ],
        out_specs=rows,
        out_shape=jax.ShapeDtypeStruct((t, D_MODEL), F32),
        compiler_params=_params(("parallel",)),
        name="post_mixer",
    )(h, mixed, p, w_out, g2, w_gu, w_gu, w_down, g_ple, w_gate, w_pproj, norm_final)


def _rope_tables(positions):
    b, s = positions.shape
    pos = positions.astype(F32)[..., None]

    def cs(rot_dim):
        inv_freq = 1.0 / (ROPE_THETA ** (jnp.arange(0, rot_dim, 2, dtype=F32) / rot_dim))
        ang = pos * inv_freq
        return jnp.cos(ang), jnp.sin(ang)

    cos_d, sin_d = cs(2 * DIFF_ROT_HALF)
    cos_m, sin_m = cs(2 * MLA_ROT_HALF)
    compact = jnp.concatenate([cos_d, sin_d, cos_m, sin_m], -1)
    tabt = jnp.swapaxes(compact, 1, 2)

    cd, sd, cm, sm, one = 0, 8, 16, 32, 48
    place = np.zeros((49, 6 * LANES), np.float32)
    for lane in range(LANES):
        r = lane % DIFF_QK
        place[cd + r % 8 if r < 16 else one, 0 * LANES + lane] = 1.0
        if 8 <= r < 16:
            place[sd + r - 8, 1 * LANES + lane] = 1.0
        if r < 8:
            place[sd + r, 2 * LANES + lane] = -1.0
        in_x1, in_x2 = 64 <= lane < 80, 80 <= lane < 96
        place[cm + (lane - 64) % 16 if (in_x1 or in_x2) else one, 3 * LANES + lane] = 1.0
        if in_x2:
            place[sm + lane - 80, 4 * LANES + lane] = 1.0
        if in_x1:
            place[sm + lane - 64, 5 * LANES + lane] = -1.0
    source = jnp.concatenate([compact, jnp.ones((b, s, 1), F32)], -1).reshape(b * s, 49)
    tabn = jnp.dot(source, jnp.asarray(place), precision=lax.Precision.HIGHEST)
    return tabn, tabt


def _prep_mixer_weights(w_in, mla_w_uq, mla_w_ukv):
    n_l = w_in.shape[0]
    seg = lambda i: w_in[:, :, _OFFS[i]:_OFFS[i + 1]]
    a_q, a_k, a_v, b_q, b_k, b_v, c_q, c_kv, k_pe = (seg(i) for i in range(9))
    zeros = lambda n: jnp.zeros((n_l, D_MODEL, n), w_in.dtype)
    k_pe_pad = jnp.concatenate([zeros(MLA_NOPE), k_pe, zeros(LANES - MLA_NOPE - MLA_ROPE)], -1)
    wnat = jnp.concatenate([a_k, b_k, c_q, c_kv, k_pe_pad], -1).astype(BF16)
    wtr = jnp.swapaxes(jnp.concatenate([a_q, a_v, b_q, b_v], -1), 1, 2).astype(BF16)
    uq = mla_w_uq.reshape(n_l, MLA_Q_RANK, MLA_HEADS, MLA_NOPE + MLA_ROPE)
    uq = jnp.pad(uq, ((0, 0), (0, 0), (0, 0), (0, LANES - MLA_NOPE - MLA_ROPE)))
    wuq = jnp.swapaxes(uq.reshape(n_l, MLA_Q_RANK, MLA_HEADS * LANES), 1, 2).astype(BF16)
    ukv = mla_w_ukv.reshape(n_l, MLA_KV_RANK, MLA_HEADS, MLA_NOPE + MLA_V)
    uk = jnp.pad(ukv[..., :MLA_NOPE], ((0, 0), (0, 0), (0, 0), (0, LANES - MLA_NOPE)))
    wuk = uk.reshape(n_l, MLA_KV_RANK, MLA_HEADS * LANES).astype(BF16)
    wuv = jnp.swapaxes(ukv[..., MLA_NOPE:].reshape(n_l, MLA_KV_RANK, MLA_HEADS * MLA_V), 1, 2).astype(BF16)
    return wnat, wtr, wuq, wuk, wuv


def kernel(x, p, positions, norm_ffn1, w_ffn1_gu, w_ffn1_down, norm_mix, w_in, mla_q_norm, mla_w_uq,
           mla_kv_norm, mla_w_ukv, diff_lambda_q1, diff_lambda_k1, diff_lambda_q2, diff_lambda_k2,
           diff_subln, w_out, norm_ffn2, w_ffn2_gu, w_ffn2_down, norm_ple, w_ple_gate, w_ple_proj,
           norm_final):
    batch, seq, _ = x.shape
    depth = w_in.shape[0]
    assert seq % ATT_TILE == 0 and ATT_TILE == ROW_TILE
    t = batch * seq
    row3 = lambda a: a.reshape(a.shape[0], 1, a.shape[1])

    tabn, tabt = _rope_tables(positions)
    wnat, wtr, wuq, wuk, wuv = _prep_mixer_weights(w_in, mla_w_uq, mla_w_ukv)
    w1_gu, w1_d = w_ffn1_gu.astype(BF16), w_ffn1_down.astype(BF16)
    w2_gu, w2_d = w_ffn2_gu.astype(BF16), w_ffn2_down.astype(BF16)
    w_out_b = w_out.astype(BF16)
    w_gate_b, w_pproj_b = w_ple_gate.astype(BF16), w_ple_proj.astype(BF16)
    g_ffn1, g_mix, g_ffn2, g_ple = row3(norm_ffn1), row3(norm_mix), row3(norm_ffn2), row3(norm_ple)
    g_q, g_kv = row3(mla_q_norm), row3(mla_kv_norm)
    lq1, lk1, lq2, lk2 = (row3(a) for a in (diff_lambda_q1, diff_lambda_k1, diff_lambda_q2, diff_lambda_k2))
    subln4 = row3(jnp.tile(diff_subln, (1, DIFF_HEADS)))
    p_rows = p.reshape(depth, t, PLE_DIM)
    fin = norm_final.reshape(1, D_MODEL)

    h = x.reshape(t, D_MODEL)
    for layer in range(depth):
        h, qta, ka, vta, qtb, kb, vtb, qtc, kc, vtc = _pre_mixer(
            h, g_ffn1, w1_gu, w1_d, g_mix, wnat, wtr, g_q, g_kv, wuq, wuk, wuv, tabn, tabt, layer, batch, seq)
        mixed = _mixer_attn(lq1, lk1, lq2, lk2, subln4, qta, ka, vta, qtb, kb, vtb, qtc, kc, vtc,
                            layer, batch, seq)
        h = _post_mixer(h, mixed.reshape(t, D_MODEL), p_rows, w_out_b, g_ffn2, w2_gu, w2_d, g_ple,
                        w_gate_b, w_pproj_b, fin, layer, final=(layer == depth - 1))
    return h.reshape(batch, seq, D_MODEL)
```

```python
import functools
import math

import jax
import jax.numpy as jnp
from jax import lax
from jax.experimental import pallas as pl
from jax.experimental.pallas import tpu as pltpu

F32 = jnp.float32
BF16 = jnp.bfloat16

D_MODEL = 1024
SB_HEADS, SB_DIM = 4, 64
DIFF_HEADS, DIFF_QK, DIFF_V = 4, 64, 128
MLA_HEADS, MLA_NOPE, MLA_ROPE, MLA_V = 4, 64, 32, 64
MLA_Q_RANK, MLA_KV_RANK = 256, 128
D_FF = 2816
PLE_DIM = 256
ROPE_THETA = 500000.0
DIFF_ROT_HALF = 8
MLA_ROT_HALF = 16
EPS = 1e-6
LOG2E = 1.4426950408889634
MASK_VALUE = -1e30
SHIFT_OFFSET = 60.0
SHIFT_LIMIT = 90.0
SHIFT_MARGIN = 1.0 + 2.0 ** -6

LANES = 128
MXU_DIM = 256
VMEM_LIMIT_BYTES = 56 * 1024 * 1024

ROW_TILE = 512
FF_CHUNK = 2 * MXU_DIM
ATT_TILE = 2 * MXU_DIM
V_PAD = 16
DIFF_VT = DIFF_V + V_PAD
MLA_VT = MLA_V + V_PAD

_SPLITS = (256, 256, 256, 512, 512, 512, MLA_Q_RANK, MLA_KV_RANK, MLA_ROPE)
_OFFS = tuple(int(sum(_SPLITS[:i])) for i in range(len(_SPLITS) + 1))
N_NAT = 256 + 512 + MLA_Q_RANK + MLA_KV_RANK + LANES
N_TR = 256 + 256 + 512 + 512


def _params(sem):
    return pltpu.CompilerParams(dimension_semantics=sem, vmem_limit_bytes=VMEM_LIMIT_BYTES)


def _rms_norm(x, gain):
    ms = jnp.mean(x * x, axis=-1, keepdims=True)
    return x * lax.rsqrt(ms + EPS) * gain


def _dot(a, b):
    return jnp.dot(a, b, preferred_element_type=F32)


def _dot_nt(a, b):
    return lax.dot_general(a, b, (((1,), (1,)), ((), ())), preferred_element_type=F32)


def _half_step_ffn(h, gain, wg_ref, wv_ref, wd_ref):
    xn = _rms_norm(h, gain).astype(BF16)
    acc = None
    for lo in range(0, D_FF, FF_CHUNK):
        hi = min(lo + FF_CHUNK, D_FF)
        g = _dot(xn, wg_ref[:, lo:hi])
        v = _dot(xn, wv_ref[:, lo:hi])
        a = (g * jax.nn.sigmoid(g) * v).astype(BF16)
        part = _dot(a, wd_ref[lo:hi, :])
        acc = part if acc is None else acc + part
    return h + 0.5 * acc


def _rope_lanes(x, c, sp, sm, shift):
    return x * c + pltpu.roll(x, shift, 1) * sp + pltpu.roll(x, LANES - shift, 1) * sm


def _rope_rows(x, base, half, cos, sin):
    x1 = x[base:base + half]
    x2 = x[base + half:base + 2 * half]
    return x1 * cos - x2 * sin, x2 * cos + x1 * sin


def _pre_mixer_kernel(h_ref, g1_ref, wg_ref, wv_ref, wd_ref,
                      gain_ref, wnat_ref, wtr_ref, qn_ref, kvn_ref, wuq_ref, wuk_ref, wuv_ref,
                      tabn_ref, tabt_ref,
                      h_out_ref, qta_ref, ka_ref, vta_ref, qtb_ref, kb_ref, vtb_ref, qtc_ref, kc_ref, vtc_ref):
    h = _half_step_ffn(h_ref[...], g1_ref[...], wg_ref, wv_ref, wd_ref)
    h_out_ref[...] = h

    u = _rms_norm(h, gain_ref[...]).astype(BF16)
    nat = _dot(u, wnat_ref[...])
    tr = _dot_nt(wtr_ref[...], u)

    tabn = tabn_ref[...]
    tabt = tabt_ref[0]
    cos_d, sin_d = tabt[0:8], tabt[8:16]
    cos_m, sin_m = tabt[16:32], tabt[32:48]

    tm = h_ref.shape[0]
    ones_row = jnp.where(lax.broadcasted_iota(jnp.int32, (V_PAD, tm), 0) == 0, 1.0, 0.0)

    def with_ones(vt, n_heads, dv):
        parts = []
        for hd in range(n_heads):
            parts += [vt[hd * dv:(hd + 1) * dv], ones_row]
        return jnp.concatenate(parts, axis=0)

    qta_ref[0] = (tr[0:256] * (SB_DIM ** -0.5 * LOG2E)).astype(BF16)
    ka_ref[0, 0] = nat[:, 0:256].astype(BF16)
    vta_ref[0, 0] = tr[256:512].astype(BF16)

    qb = tr[512:1024]
    parts = []
    for ch in range(2 * DIFF_HEADS):
        base = ch * DIFF_QK
        r1, r2 = _rope_rows(qb, base, DIFF_ROT_HALF, cos_d, sin_d)
        parts += [r1, r2, qb[base + 2 * DIFF_ROT_HALF:base + DIFF_QK]]
    qtb_ref[0] = (jnp.concatenate(parts, axis=0) * (DIFF_QK ** -0.5 * LOG2E)).astype(BF16)
    kb = nat[:, 256:768]
    c_d, sp_d, sm_d = tabn[:, 0:128], tabn[:, 128:256], tabn[:, 256:384]
    kb = jnp.concatenate(
        [_rope_lanes(kb[:, c * LANES:(c + 1) * LANES], c_d, sp_d, sm_d, DIFF_ROT_HALF) for c in range(4)], axis=1)
    kb_ref[0, 0] = kb.astype(BF16)
    vtb_ref[0, 0] = with_ones(tr[1024:1536], DIFF_HEADS, DIFF_V).astype(BF16)

    cq = _rms_norm(nat[:, 768:1024], qn_ref[...]).astype(BF16)
    qc = _dot_nt(wuq_ref[...], cq)
    parts = []
    for hd in range(MLA_HEADS):
        base = hd * LANES
        r1, r2 = _rope_rows(qc, base + MLA_NOPE, MLA_ROT_HALF, cos_m, sin_m)
        parts += [qc[base:base + MLA_NOPE], r1, r2, qc[base + MLA_NOPE + MLA_ROPE:base + LANES]]
    qtc_ref[0] = (jnp.concatenate(parts, axis=0) * ((MLA_NOPE + MLA_ROPE) ** -0.5 * LOG2E)).astype(BF16)
    ckv = _rms_norm(nat[:, 1024:1152], kvn_ref[...]).astype(BF16)
    k_nope = _dot(ckv, wuk_ref[...])
    c_m, sp_m, sm_m = tabn[:, 384:512], tabn[:, 512:640], tabn[:, 640:768]
    k_rot = _rope_lanes(nat[:, 1152:1280], c_m, sp_m, sm_m, MLA_ROT_HALF)
    kc = jnp.concatenate([k_nope[:, hd * LANES:(hd + 1) * LANES] + k_rot for hd in range(MLA_HEADS)], axis=1)
    kc_ref[0, 0] = kc.astype(BF16)
    vtc_ref[0, 0] = with_ones(_dot_nt(wuv_ref[...], ckv), MLA_HEADS, MLA_V).astype(BF16)


def _pre_mixer(h, g1, w_gu, w_down, gain, wnat, wtr, qn, kvn, wuq, wuk, wuv, tabn, tabt, layer, batch, seq):
    t = h.shape[0]
    nblk = seq // ATT_TILE
    resident = pl.Buffered(1)

    def row_blocked(n):
        return (jax.ShapeDtypeStruct((batch, nblk, ATT_TILE, n), BF16),
                pl.BlockSpec((1, 1, ATT_TILE, n), lambda b, i: (b, i, 0, 0)))

    def col_blocked(n):
        return (jax.ShapeDtypeStruct((batch, nblk, n, ATT_TILE), BF16),
                pl.BlockSpec((1, 1, n, ATT_TILE), lambda b, i: (b, i, 0, 0)))

    def q_t(n):
        return (jax.ShapeDtypeStruct((batch, n, seq), BF16),
                pl.BlockSpec((1, n, ATT_TILE), lambda b, i: (b, 0, i)))

    rows = pl.BlockSpec((ATT_TILE, D_MODEL), lambda b, i: (b * nblk + i, 0))
    outs = [(jax.ShapeDtypeStruct((t, D_MODEL), F32), rows),
            q_t(256), row_blocked(256), col_blocked(256),
            q_t(512), row_blocked(512), col_blocked(DIFF_HEADS * DIFF_VT),
            q_t(512), row_blocked(512), col_blocked(MLA_HEADS * MLA_VT)]
    whole = lambda shape: pl.BlockSpec((None,) + shape, lambda b, i: (layer,) + (0,) * len(shape),
                                       pipeline_mode=resident)
    return pl.pallas_call(
        _pre_mixer_kernel,
        grid=(batch, nblk),
        in_specs=[
            rows,
            whole((1, D_MODEL)),
            pl.BlockSpec((None, D_MODEL, D_FF), lambda b, i: (layer, 0, 0), pipeline_mode=resident),
            pl.BlockSpec((None, D_MODEL, D_FF), lambda b, i: (layer, 0, 1), pipeline_mode=resident),
            whole((D_FF, D_MODEL)),
            whole((1, D_MODEL)),
            whole((D_MODEL, N_NAT)),
            whole((N_TR, D_MODEL)),
            whole((1, MLA_Q_RANK)),
            whole((1, MLA_KV_RANK)),
            whole((4 * LANES, MLA_Q_RANK)),
            whole((MLA_KV_RANK, 4 * LANES)),
            whole((MLA_HEADS * MLA_V, MLA_KV_RANK)),
            pl.BlockSpec((ATT_TILE, 6 * LANES), lambda b, i: (b * nblk + i, 0)),
            pl.BlockSpec((1, 48, ATT_TILE), lambda b, i: (b, 0, i)),
        ],
        out_specs=[o[1] for o in outs],
        out_shape=[o[0] for o in outs],
        compiler_params=_params(("parallel", "parallel")),
        name="pre_mixer",
    )(h, g1, w_gu, w_gu, w_down, gain, wnat, wtr, qn, kvn, wuq, wuk, wuv, tabn, tabt)


def _row_mask(n_rows, lo, hi, n_cols):
    r = lax.broadcasted_iota(jnp.int32, (n_rows, n_cols), 0)
    return jnp.where((r >= lo) & (r < hi), 1.0, 0.0).astype(BF16)


def _emit_skewed(items):
    n_steps = max(k + len(stages) for k, stages in enumerate(items))
    for step in range(n_steps):
        for k, stages in reversed(list(enumerate(items))):
            s_idx = step - k
            if 0 <= s_idx < len(stages):
                stages[s_idx]()


def _softmax_items(k_blk, vt_blk, q_heads, kv_heads, m_ref, acc_ref, vt_rows, visible):
    items = []
    for hd in range(len(q_heads)):
        st = {}
        kv = kv_heads[hd]

        def scores(hd=hd, st=st, kv=kv):
            s = _dot(k_blk[:, kv * LANES:(kv + 1) * LANES], q_heads[hd])
            st["s"] = s if visible is None else jnp.where(visible, s, MASK_VALUE)

        def probs(hd=hd, st=st):
            s = st.pop("s")
            m_old = m_ref[hd]
            m_new = jnp.maximum(m_old, jnp.max(s, axis=0, keepdims=True))
            st["alpha"] = jnp.exp2(m_old - m_new)
            st["p"] = jnp.exp2(s - m_new).astype(BF16)
            m_ref[hd] = m_new

        def accumulate(hd=hd, st=st, kv=kv):
            pv = _dot(vt_blk[kv * vt_rows:(kv + 1) * vt_rows], st.pop("p"))
            acc_ref[hd] = st.pop("alpha") * acc_ref[hd] + pv

        items.append([scores, probs, accumulate])
    return items


def _shifted_softmax_items(k_blk, vt_blk, q_aug, kv_heads, acc_ref, vt_rows, ones_col, diag_masks):
    half = MXU_DIM
    items = []
    for hd in range(len(q_aug)):
        st = {}
        kv = kv_heads[hd]

        def scores(hd=hd, st=st, kv=kv):
            k_aug = jnp.concatenate([k_blk[:, kv * LANES:(kv + 1) * LANES], ones_col], axis=1)
            if diag_masks is None:
                st["s"] = (_dot(k_aug, q_aug[hd]),)
            else:
                wide, square = diag_masks
                st["s"] = (jnp.where(wide, _dot(k_aug[:half], q_aug[hd]), MASK_VALUE),
                           jnp.where(square, _dot(k_aug[half:], q_aug[hd][:, half:]), MASK_VALUE))

        def probs(st=st):
            st["p"] = tuple(jnp.exp2(s).astype(BF16) for s in st.pop("s"))

        def accumulate(hd=hd, st=st, kv=kv):
            vt = vt_blk[kv * vt_rows:(kv + 1) * vt_rows]
            p = st.pop("p")
            if diag_masks is None:
                acc_ref[hd] += _dot(vt, p[0])
            else:
                acc_ref[hd] = _dot(vt[:, :half], p[0])
                acc_ref[hd, :, half:] += _dot(vt[:, half:], p[1])

        items.append([scores, probs, accumulate])
    return items


def _log2_sigmoid_terms(y):
    neg_abs = pltpu.bitcast(pltpu.bitcast(y, jnp.uint32) | jnp.uint32(0x80000000), F32)
    log_beta = jnp.minimum(y, 0.0) - jnp.log2(1.0 + jnp.exp2(neg_abs))
    return log_beta, log_beta - y


def _stick_breaking_items(k_blk, vt_blk, q_heads, c_ref, acc_ref, upper):
    half = MXU_DIM
    items = []
    for hd in range(len(q_heads)):
        st = {}

        def logits(hd=hd, st=st):
            st["y"] = _dot(k_blk[:, (hd // 2) * LANES:(hd // 2 + 1) * LANES], q_heads[hd])

        def log_terms(st=st):
            log_beta, log_keep = _log2_sigmoid_terms(st.pop("y"))
            st.update(log_beta=log_beta, keep=log_keep.astype(BF16),
                      first=log_keep[0:1], mid=log_keep[half:half + 1])

        def suffix_sums(st=st):
            keep = st.pop("keep")
            st["later"] = _dot(upper, keep[half:])
            st["earlier"] = _dot(upper, keep[:half])

        def weights(hd=hd, st=st):
            later, earlier = st.pop("later"), st.pop("earlier")
            c_old = c_ref[hd]
            later_total = later[0:1] + st.pop("mid") + c_old
            between = jnp.concatenate([earlier + later_total, later + c_old], axis=0)
            st["a"] = jnp.exp2(st.pop("log_beta") + between).astype(BF16)
            c_ref[hd] = (earlier[0:1] + st.pop("first")) + later_total

        def accumulate(hd=hd, st=st):
            acc_ref[hd] += _dot(vt_blk[hd * SB_DIM:(hd + 1) * SB_DIM], st.pop("a"))

        items.append([logits, log_terms, suffix_sums, weights, accumulate])
    return items


def _stick_breaking_diag_items(k_blk, vt_blk, q_heads, c_ref, acc_ref, upper, masks):
    half = MXU_DIM
    items = []
    for hd in range(len(q_heads)):
        st = {}

        def logits(hd=hd, st=st):
            kh = k_blk[:, (hd // 2) * LANES:(hd // 2 + 1) * LANES]
            st["y"] = (_dot(kh[:half], q_heads[hd]), _dot(kh[half:], q_heads[hd][:, half:]))

        def log_terms(st=st):
            out = []
            for y, mask in zip(st.pop("y"), masks):
                log_beta, log_keep = _log2_sigmoid_terms(y)
                log_keep = jnp.where(mask, log_keep, 0.0)
                out.append((log_beta, log_keep.astype(BF16), log_keep[0:1]))
            st["terms"] = out

        def suffix_sums(st=st):
            st["sums"] = tuple(_dot(upper, keep) for _, keep, _ in st["terms"])

        def weights(hd=hd, st=st):
            earlier, later = st.pop("sums")
            (lb_e, _, first_e), (lb_l, _, first_l) = st.pop("terms")
            later_total = later[0:1] + first_l
            later_total = jnp.concatenate([jnp.zeros_like(later_total), later_total], axis=1)
            a_earlier = jnp.where(masks[0], jnp.exp2(lb_e + (earlier + later_total)), 0.0)
            a_later = jnp.where(masks[1], jnp.exp2(lb_l + later), 0.0)
            c_ref[hd] = (earlier[0:1] + first_e) + later_total
            st["a"] = (a_earlier.astype(BF16), a_later.astype(BF16))

        def accumulate(hd=hd, st=st):
            vt = vt_blk[hd * SB_DIM:(hd + 1) * SB_DIM]
            a_earlier, a_later = st.pop("a")
            acc_ref[hd] = _dot(vt[:, :half], a_earlier)
            acc_ref[hd, :, half:] += _dot(vt[:, half:], a_later)

        items.append([logits, log_terms, suffix_sums, weights, accumulate])
    return items


def _mixer_attn_kernel(lq1_ref, lk1_ref, lq2_ref, lk2_ref, subln_ref,
                       qta_ref, ka_ref, vta_ref, qtb_ref, kb_ref, vtb_ref, qtc_ref, kc_ref, vtc_ref,
                       *refs, lambda_init, shifted, nq):
    i = pl.program_id(1)
    ta = ATT_TILE
    if shifted:
        o_ref, worst_ref, c_ref, acc_a_ref, acc_b_ref, acc_c_ref, kmax_b_ref, kmax_c_ref = refs
    else:
        o_ref, c_ref, acc_a_ref, m_b_ref, acc_b_ref, m_c_ref, acc_c_ref = refs
        m_b_ref[...] = jnp.full_like(m_b_ref, MASK_VALUE)
        m_c_ref[...] = jnp.full_like(m_c_ref, MASK_VALUE)
        acc_b_ref[...] = jnp.zeros_like(acc_b_ref)
        acc_c_ref[...] = jnp.zeros_like(acc_c_ref)

    qta, qtb, qtc = qta_ref[0], qtb_ref[0], qtc_ref[0]
    q_a, q_b, q_c = [], [], []
    for hd in range(SB_HEADS):
        lo = (hd % 2) * SB_DIM
        q_a.append(qta[(hd // 2) * LANES:(hd // 2 + 1) * LANES] * _row_mask(LANES, lo, lo + SB_DIM, ta))
    for hd in range(DIFF_HEADS):
        qh = qtb[hd * LANES:(hd + 1) * LANES]
        q_b += [qh * _row_mask(LANES, 0, DIFF_QK, ta), qh * _row_mask(LANES, DIFF_QK, LANES, ta)]
    for hd in range(MLA_HEADS):
        q_c.append(qtc[hd * LANES:(hd + 1) * LANES])

    if shifted:
        @pl.when(i == 0)
        def _():
            for k_ref, kmax_ref in ((kb_ref, kmax_b_ref), (kc_ref, kmax_c_ref)):
                kmax = jnp.zeros((1, k_ref.shape[-1]), F32)
                for jb in range(nq):
                    kmax = jnp.maximum(kmax, jnp.max(jnp.abs(k_ref[0, jb].astype(F32)), axis=0, keepdims=True))
                kmax_ref[...] = kmax

        ones_col = jnp.where(lax.broadcasted_iota(jnp.int32, (ta, LANES), 1) == 0, 1.0, 0.0).astype(BF16)
        first_row = lax.broadcasted_iota(jnp.int32, (LANES, ta), 0) == 0
        first_of_16 = lax.broadcasted_iota(jnp.int32, (16, LANES), 0) == 0
        worst = jnp.zeros((1, 1), F32)

        def augment(q, kmax_row):
            kmax_tile = jnp.where(first_of_16, kmax_row, 0.0).astype(BF16)
            bound = _dot(kmax_tile, jnp.abs(q))[0:1]
            shift = bound * SHIFT_MARGIN
            extra = jnp.where(first_row, SHIFT_OFFSET - shift, 0.0).astype(BF16)
            return jnp.concatenate([q, extra], axis=0), jnp.max(shift, axis=1, keepdims=True)

        qb_aug, qc_aug = [], []
        for n, q in enumerate(q_b):
            kv = n // 2
            q_aug, top = augment(q, kmax_b_ref[:, kv * LANES:(kv + 1) * LANES])
            qb_aug.append(q_aug)
            worst = jnp.maximum(worst, top)
        for n, q in enumerate(q_c):
            q_aug, top = augment(q, kmax_c_ref[:, n * LANES:(n + 1) * LANES])
            qc_aug.append(q_aug)
            worst = jnp.maximum(worst, top)
        worst_ref[...] = jnp.broadcast_to(worst, worst_ref.shape)
    half = MXU_DIM
    kk = lax.broadcasted_iota(jnp.int32, (half, ta), 0)
    qq = lax.broadcasted_iota(jnp.int32, (half, ta), 1)
    kk_sq = lax.broadcasted_iota(jnp.int32, (half, half), 0)
    qq_sq = lax.broadcasted_iota(jnp.int32, (half, half), 1)
    visible = (kk <= qq, kk_sq <= qq_sq)
    strictly_before = (kk < qq, kk_sq < qq_sq)
    kk_full = lax.broadcasted_iota(jnp.int32, (ta, ta), 0)
    qq_full = lax.broadcasted_iota(jnp.int32, (ta, ta), 1)
    upper = jnp.where(qq_sq > kk_sq, 1.0, 0.0).astype(BF16)

    def block(j, diag):
        if diag:
            a_items = _stick_breaking_diag_items(ka_ref[0, j], vta_ref[0, j], q_a, c_ref, acc_a_ref, upper,
                                                 strictly_before)
        else:
            a_items = _stick_breaking_items(ka_ref[0, j], vta_ref[0, j], q_a, c_ref, acc_a_ref, upper)
        kv_b, kv_c = [n // 2 for n in range(2 * DIFF_HEADS)], list(range(MLA_HEADS))
        if shifted:
            masks = visible if diag else None
            b_items = _shifted_softmax_items(kb_ref[0, j], vtb_ref[0, j], qb_aug, kv_b, acc_b_ref, DIFF_VT,
                                             ones_col, masks)
            c_items = _shifted_softmax_items(kc_ref[0, j], vtc_ref[0, j], qc_aug, kv_c, acc_c_ref, MLA_VT,
                                             ones_col, masks)
        else:
            mask = (kk_full <= qq_full) if diag else None
            b_items = _softmax_items(kb_ref[0, j], vtb_ref[0, j], q_b, kv_b, m_b_ref, acc_b_ref, DIFF_VT, mask)
            c_items = _softmax_items(kc_ref[0, j], vtc_ref[0, j], q_c, kv_c, m_c_ref, acc_c_ref, MLA_VT, mask)
        items = []
        for hd in range(SB_HEADS):
            items += [b_items[2 * hd], a_items[hd], c_items[hd], b_items[2 * hd + 1]]
        _emit_skewed(items)

    block(i, True)

    def body(jj, carry):
        block(i - 1 - jj, False)
        return carry

    lax.fori_loop(0, i, body, 0)

    o_ref[0, :, 0:256] = acc_a_ref[...].reshape(SB_HEADS * SB_DIM, ta).T.astype(o_ref.dtype)

    lam = (jnp.exp(jnp.sum(lq1_ref[...] * lk1_ref[...], axis=-1, keepdims=True))
           - jnp.exp(jnp.sum(lq2_ref[...] * lk2_ref[...], axis=-1, keepdims=True)) + lambda_init)
    outs = []
    for hd in range(DIFF_HEADS):
        acc1, acc2 = acc_b_ref[2 * hd], acc_b_ref[2 * hd + 1]
        o = (acc1[0:DIFF_V] * (1.0 / acc1[DIFF_V:DIFF_V + 1])
             - acc2[0:DIFF_V] * (lam / acc2[DIFF_V:DIFF_V + 1]))
        ms = jnp.mean(o * o, axis=0, keepdims=True)
        outs.append(o * lax.rsqrt(ms + EPS))
    o_nat = jnp.concatenate(outs, axis=0).T
    o_ref[0, :, 256:768] = (o_nat * subln_ref[...] * (1.0 - lambda_init)).astype(o_ref.dtype)

    outs = []
    for hd in range(MLA_HEADS):
        acc = acc_c_ref[hd]
        outs.append(acc[0:MLA_V] * (1.0 / acc[MLA_V:MLA_V + 1]))
    o_ref[0, :, 768:1024] = jnp.concatenate(outs, axis=0).T.astype(o_ref.dtype)


def _mixer_attn(lq1, lk1, lq2, lk2, subln4, qta, ka, vta, qtb, kb, vtb, qtc, kc, vtc, layer, batch, seq):
    nq = seq // ATT_TILE
    lambda_init = 0.8 - 0.6 * math.exp(-0.3 * layer)
    vec = lambda n: pl.BlockSpec((None, 1, n), lambda b, i: (layer, 0, 0))
    q_t = lambda n: pl.BlockSpec((1, n, ATT_TILE), lambda b, i: (b, 0, i))
    per_batch = lambda r, c: pl.BlockSpec((1, nq, r, c), lambda b, i: (b, 0, 0, 0),
                                          pipeline_mode=pl.Buffered(1))
    operands = (lq1, lk1, lq2, lk2, subln4, qta, ka, vta, qtb, kb, vtb, qtc, kc, vtc)
    in_specs = [
        vec(DIFF_QK), vec(DIFF_QK), vec(DIFF_QK), vec(DIFF_QK), vec(DIFF_HEADS * DIFF_V),
        q_t(256), per_batch(ATT_TILE, 256), per_batch(256, ATT_TILE),
        q_t(512), per_batch(ATT_TILE, 512), per_batch(DIFF_HEADS * DIFF_VT, ATT_TILE),
        q_t(512), per_batch(ATT_TILE, 512), per_batch(MLA_HEADS * MLA_VT, ATT_TILE),
    ]
    mixed_spec = pl.BlockSpec((1, ATT_TILE, D_MODEL), lambda b, i: (b, i, 0))
    mixed_shape = jax.ShapeDtypeStruct((batch, seq, D_MODEL), BF16)
    state = lambda n, rows: pltpu.VMEM((n, rows, ATT_TILE), F32)

    def call(shifted):
        if shifted:
            out_specs = [mixed_spec, pl.BlockSpec((1, 1, 8, LANES), lambda b, i: (b, i, 0, 0))]
            out_shape = [mixed_shape, jax.ShapeDtypeStruct((batch, nq, 8, LANES), F32)]
            scratch = [state(SB_HEADS, 1), state(SB_HEADS, SB_DIM), state(2 * DIFF_HEADS, DIFF_VT),
                       state(MLA_HEADS, MLA_VT), pltpu.VMEM((1, 512), F32), pltpu.VMEM((1, 512), F32)]
        else:
            out_specs, out_shape = mixed_spec, mixed_shape
            scratch = [state(SB_HEADS, 1), state(SB_HEADS, SB_DIM), state(2 * DIFF_HEADS, 1),
                       state(2 * DIFF_HEADS, DIFF_VT), state(MLA_HEADS, 1), state(MLA_HEADS, MLA_VT)]
        return pl.pallas_call(
            functools.partial(_mixer_attn_kernel, lambda_init=lambda_init, shifted=shifted, nq=nq),
            grid=(batch, nq),
            in_specs=in_specs,
            out_specs=out_specs,
            out_shape=out_shape,
            scratch_shapes=scratch,
            compiler_params=_params(("arbitrary", "arbitrary")),
            name="mixer_attn" if shifted else "mixer_attn_online",
        )(*operands)

    mixed, worst = call(True)
    return lax.cond(jnp.max(worst) > SHIFT_LIMIT, lambda: call(False), lambda: mixed)


def _post_mixer_kernel(h_ref, mix_ref, p_ref, wo_ref, g2_ref, wg_ref, wv_ref, wd_ref,
                       gp_ref, wgate_ref, wpp_ref, fin_ref, o_ref, *, final):
    h = h_ref[...] + _dot(mix_ref[...], wo_ref[...])
    h = _half_step_ffn(h, g2_ref[...], wg_ref, wv_ref, wd_ref)
    gate = jax.nn.sigmoid(_dot(_rms_norm(h, gp_ref[...]).astype(BF16), wgate_ref[...]))
    out = h + _dot(p_ref[0].astype(BF16), wpp_ref[...]) * gate
    if final:
        out = _rms_norm(out, fin_ref[...])
    o_ref[...] = out


def _post_mixer(h, mixed, p, w_out, g2, w_gu, w_down, g_ple, w_gate, w_pproj, norm_final, layer, final):
    t = h.shape[0]
    resident = pl.Buffered(1)
    whole = lambda shape: pl.BlockSpec((None,) + shape, lambda i: (layer,) + (0,) * len(shape),
                                       pipeline_mode=resident)
    rows = pl.BlockSpec((ROW_TILE, D_MODEL), lambda i: (i, 0))
    return pl.pallas_call(
        functools.partial(_post_mixer_kernel, final=final),
        grid=(t // ROW_TILE,),
        in_specs=[
            rows, rows,
            pl.BlockSpec((1, ROW_TILE, PLE_DIM), lambda i: (layer, i, 0)),
            whole((D_MODEL, D_MODEL)),
            whole((1, D_MODEL)),
            pl.BlockSpec((None, D_MODEL, D_FF), lambda i: (layer, 0, 0), pipeline_mode=resident),
            pl.BlockSpec((None, D_MODEL, D_FF), lambda i: (layer, 0, 1), pipeline_mode=resident),
            whole((D_FF, D_MODEL)),
            whole((1, D_MODEL)),
            whole((D_MODEL, D_MODEL)),
            whole((PLE_DIM, D_MODEL)),
            pl.BlockSpec((1, D_MODEL), lambda i: (0, 0)),
        ],
        out_specs=rows,
        out_shape=jax.ShapeDtypeStruct((t, D_MODEL), F32),
        compiler_params=_params(("parallel",)),
        name="post_mixer",
    )(h, mixed, p, w_out, g2, w_gu, w_gu, w_down, g_ple, w_gate, w_pproj, norm_final)


def _rope_tables(positions):
    b, s = positions.shape
    pos = positions.astype(F32)[..., None]

    def cs(rot_dim):
        inv_freq = 1.0 / (ROPE_THETA ** (jnp.arange(0, rot_dim, 2, dtype=F32) / rot_dim))
        ang = pos * inv_freq
        return jnp.cos(ang), jnp.sin(ang)

    cos_d, sin_d = cs(2 * DIFF_ROT_HALF)
    cos_m, sin_m = cs(2 * MLA_ROT_HALF)
    one = lambda n: jnp.ones((b, s, n), F32)
    zero = lambda n: jnp.zeros((b, s, n), F32)
    c_d = jnp.tile(jnp.concatenate([cos_d, cos_d, one(48)], -1), (1, 1, 2))
    sp_d = jnp.tile(jnp.concatenate([zero(8), sin_d, zero(48)], -1), (1, 1, 2))
    sm_d = jnp.tile(jnp.concatenate([-sin_d, zero(56)], -1), (1, 1, 2))
    c_m = jnp.concatenate([one(64), cos_m, cos_m, one(32)], -1)
    sp_m = jnp.concatenate([zero(80), sin_m, zero(32)], -1)
    sm_m = jnp.concatenate([zero(64), -sin_m, zero(48)], -1)
    tabn = jnp.concatenate([c_d, sp_d, sm_d, c_m, sp_m, sm_m], -1).reshape(b * s, 6 * LANES)
    tabt = jnp.swapaxes(jnp.concatenate([cos_d, sin_d, cos_m, sin_m], -1), 1, 2)
    return tabn, tabt


def _prep_mixer_weights(w_in, mla_w_uq, mla_w_ukv):
    n_l = w_in.shape[0]
    seg = lambda i: w_in[:, :, _OFFS[i]:_OFFS[i + 1]]
    a_q, a_k, a_v, b_q, b_k, b_v, c_q, c_kv, k_pe = (seg(i) for i in range(9))
    zeros = lambda n: jnp.zeros((n_l, D_MODEL, n), w_in.dtype)
    k_pe_pad = jnp.concatenate([zeros(MLA_NOPE), k_pe, zeros(LANES - MLA_NOPE - MLA_ROPE)], -1)
    wnat = jnp.concatenate([a_k, b_k, c_q, c_kv, k_pe_pad], -1).astype(BF16)
    wtr = jnp.swapaxes(jnp.concatenate([a_q, a_v, b_q, b_v], -1), 1, 2).astype(BF16)
    uq = mla_w_uq.reshape(n_l, MLA_Q_RANK, MLA_HEADS, MLA_NOPE + MLA_ROPE)
    uq = jnp.pad(uq, ((0, 0), (0, 0), (0, 0), (0, LANES - MLA_NOPE - MLA_ROPE)))
    wuq = jnp.swapaxes(uq.reshape(n_l, MLA_Q_RANK, MLA_HEADS * LANES), 1, 2).astype(BF16)
    ukv = mla_w_ukv.reshape(n_l, MLA_KV_RANK, MLA_HEADS, MLA_NOPE + MLA_V)
    uk = jnp.pad(ukv[..., :MLA_NOPE], ((0, 0), (0, 0), (0, 0), (0, LANES - MLA_NOPE)))
    wuk = uk.reshape(n_l, MLA_KV_RANK, MLA_HEADS * LANES).astype(BF16)
    wuv = jnp.swapaxes(ukv[..., MLA_NOPE:].reshape(n_l, MLA_KV_RANK, MLA_HEADS * MLA_V), 1, 2).astype(BF16)
    return wnat, wtr, wuq, wuk, wuv


def kernel(x, p, positions, norm_ffn1, w_ffn1_gu, w_ffn1_down, norm_mix, w_in, mla_q_norm, mla_w_uq,
           mla_kv_norm, mla_w_ukv, diff_lambda_q1, diff_lambda_k1, diff_lambda_q2, diff_lambda_k2,
           diff_subln, w_out, norm_ffn2, w_ffn2_gu, w_ffn2_down, norm_ple, w_ple_gate, w_ple_proj,
           norm_final):
    batch, seq, _ = x.shape
    depth = w_in.shape[0]
    assert seq % ATT_TILE == 0 and ATT_TILE == ROW_TILE
    t = batch * seq
    row3 = lambda a: a.reshape(a.shape[0], 1, a.shape[1])

    tabn, tabt = _rope_tables(positions)
    wnat, wtr, wuq, wuk, wuv = _prep_mixer_weights(w_in, mla_w_uq, mla_w_ukv)
    w1_gu, w1_d = w_ffn1_gu.astype(BF16), w_ffn1_down.astype(BF16)
    w2_gu, w2_d = w_ffn2_gu.astype(BF16), w_ffn2_down.astype(BF16)
    w_out_b = w_out.astype(BF16)
    w_gate_b, w_pproj_b = w_ple_gate.astype(BF16), w_ple_proj.astype(BF16)
    g_ffn1, g_mix, g_ffn2, g_ple = row3(norm_ffn1), row3(norm_mix), row3(norm_ffn2), row3(norm_ple)
    g_q, g_kv = row3(mla_q_norm), row3(mla_kv_norm)
    lq1, lk1, lq2, lk2 = (row3(a) for a in (diff_lambda_q1, diff_lambda_k1, diff_lambda_q2, diff_lambda_k2))
    subln4 = row3(jnp.tile(diff_subln, (1, DIFF_HEADS)))
    p_rows = p.reshape(depth, t, PLE_DIM)
    fin = norm_final.reshape(1, D_MODEL)

    h = x.reshape(t, D_MODEL)
    for layer in range(depth):
        h, qta, ka, vta, qtb, kb, vtb, qtc, kc, vtc = _pre_mixer(
            h, g_ffn1, w1_gu, w1_d, g_mix, wnat, wtr, g_q, g_kv, wuq, wuk, wuv, tabn, tabt, layer, batch, seq)
        mixed = _mixer_attn(lq1, lk1, lq2, lk2, subln4, qta, ka, vta, qtb, kb, vtb, qtc, kc, vtc,
                            layer, batch, seq)
        h = _post_mixer(h, mixed.reshape(t, D_MODEL), p_rows, w_out_b, g_ffn2, w2_gu, w2_d, g_ple,
                        w_gate_b, w_pproj_b, fin, layer, final=(layer == depth - 1))
    return h.reshape(batch, seq, D_MODEL)
```

```python
import functools
import math

import jax
import jax.numpy as jnp
from jax import lax
from jax.experimental import pallas as pl
from jax.experimental.pallas import tpu as pltpu

F32 = jnp.float32
BF16 = jnp.bfloat16

D_MODEL = 1024
SB_HEADS, SB_DIM = 4, 64
DIFF_HEADS, DIFF_QK, DIFF_V = 4, 64, 128
MLA_HEADS, MLA_NOPE, MLA_ROPE, MLA_V = 4, 64, 32, 64
MLA_Q_RANK, MLA_KV_RANK = 256, 128
D_FF = 2816
PLE_DIM = 256
ROPE_THETA = 500000.0
DIFF_ROT_HALF = 8
MLA_ROT_HALF = 16
EPS = 1e-6
LOG2E = 1.4426950408889634
MASK_VALUE = -1e30
SHIFT_OFFSET = 60.0
SHIFT_LIMIT = 90.0
SHIFT_MARGIN = 1.0 + 2.0 ** -6

LANES = 128
MXU_DIM = 256
VMEM_LIMIT_BYTES = 56 * 1024 * 1024

ROW_TILE = 512
FF_CHUNK = 2 * MXU_DIM
ATT_TILE = 2 * MXU_DIM
V_PAD = 16
DIFF_VT = DIFF_V + V_PAD
MLA_VT = MLA_V + V_PAD

_SPLITS = (256, 256, 256, 512, 512, 512, MLA_Q_RANK, MLA_KV_RANK, MLA_ROPE)
_OFFS = tuple(int(sum(_SPLITS[:i])) for i in range(len(_SPLITS) + 1))
N_NAT = 256 + MLA_Q_RANK + MLA_KV_RANK
N_TR = 256 + 256 + 512 + 512 + 512 + MLA_ROPE


def _params(sem):
    return pltpu.CompilerParams(dimension_semantics=sem, vmem_limit_bytes=VMEM_LIMIT_BYTES)


def _rms_norm(x, gain):
    ms = jnp.mean(x * x, axis=-1, keepdims=True)
    return x * lax.rsqrt(ms + EPS) * gain


def _dot(a, b):
    return jnp.dot(a, b, preferred_element_type=F32)


def _dot_nt(a, b):
    return lax.dot_general(a, b, (((1,), (1,)), ((), ())), preferred_element_type=F32)


def _half_step_ffn(h, gain, wg_ref, wv_ref, wd_ref):
    xn = _rms_norm(h, gain).astype(BF16)
    acc = None
    for lo in range(0, D_FF, FF_CHUNK):
        hi = min(lo + FF_CHUNK, D_FF)
        g = _dot(xn, wg_ref[:, lo:hi])
        v = _dot(xn, wv_ref[:, lo:hi])
        a = (g * jax.nn.sigmoid(g) * v).astype(BF16)
        part = _dot(a, wd_ref[lo:hi, :])
        acc = part if acc is None else acc + part
    return h + 0.5 * acc


def _rope_rows(x, base, half, cos, sin):
    x1 = x[base:base + half]
    x2 = x[base + half:base + 2 * half]
    return x1 * cos - x2 * sin, x2 * cos + x1 * sin


def _pre_mixer_kernel(h_ref, g1_ref, wg_ref, wv_ref, wd_ref,
                      gain_ref, wnat_ref, wtr_ref, qn_ref, kvn_ref, wuq_ref, wuk_ref, wuv_ref, tabt_ref,
                      h_out_ref, qta_ref, ka_ref, vta_ref, qtb_ref, kb_ref, vtb_ref, qtc_ref, kc_ref, vtc_ref):
    h = _half_step_ffn(h_ref[...], g1_ref[...], wg_ref, wv_ref, wd_ref)
    h_out_ref[...] = h

    u = _rms_norm(h, gain_ref[...]).astype(BF16)
    nat = _dot(u, wnat_ref[...])
    tr = _dot_nt(wtr_ref[...], u)

    tabt = tabt_ref[0]
    cos_d, sin_d = tabt[0:8], tabt[8:16]
    cos_m, sin_m = tabt[16:32], tabt[32:48]

    def diff_rotary(xt):
        parts = []
        for ch in range(2 * DIFF_HEADS):
            base = ch * DIFF_QK
            r1, r2 = _rope_rows(xt, base, DIFF_ROT_HALF, cos_d, sin_d)
            parts += [r1, r2, xt[base + 2 * DIFF_ROT_HALF:base + DIFF_QK]]
        return jnp.concatenate(parts, axis=0)

    tm = h_ref.shape[0]
    ones_row = jnp.where(lax.broadcasted_iota(jnp.int32, (V_PAD, tm), 0) == 0, 1.0, 0.0)

    def with_ones(vt, n_heads, dv):
        parts = []
        for hd in range(n_heads):
            parts += [vt[hd * dv:(hd + 1) * dv], ones_row]
        return jnp.concatenate(parts, axis=0)

    qta_ref[0] = (tr[0:256] * (SB_DIM ** -0.5 * LOG2E)).astype(BF16)
    ka_ref[0, 0] = nat[:, 0:256].astype(BF16)
    vta_ref[0, 0] = tr[256:512].astype(BF16)

    qtb_ref[0] = (diff_rotary(tr[512:1024]) * (DIFF_QK ** -0.5 * LOG2E)).astype(BF16)
    kb_ref[0, 0] = diff_rotary(tr[1536:2048]).T.astype(BF16)
    vtb_ref[0, 0] = with_ones(tr[1024:1536], DIFF_HEADS, DIFF_V).astype(BF16)

    cq = _rms_norm(nat[:, 256:512], qn_ref[...]).astype(BF16)
    qc = _dot_nt(wuq_ref[...], cq)
    parts = []
    for hd in range(MLA_HEADS):
        base = hd * LANES
        r1, r2 = _rope_rows(qc, base + MLA_NOPE, MLA_ROT_HALF, cos_m, sin_m)
        parts += [qc[base:base + MLA_NOPE], r1, r2, qc[base + MLA_NOPE + MLA_ROPE:base + LANES]]
    qtc_ref[0] = (jnp.concatenate(parts, axis=0) * ((MLA_NOPE + MLA_ROPE) ** -0.5 * LOG2E)).astype(BF16)
    ckv = _rms_norm(nat[:, 512:640], kvn_ref[...]).astype(BF16)
    k_nope_t = _dot_nt(wuk_ref[...], ckv)
    r1, r2 = _rope_rows(tr[2048:2080], 0, MLA_ROT_HALF, cos_m, sin_m)
    pad = jnp.zeros((LANES - MLA_NOPE - MLA_ROPE, tm), F32)
    parts = []
    for hd in range(MLA_HEADS):
        parts += [k_nope_t[hd * MLA_NOPE:(hd + 1) * MLA_NOPE], r1, r2, pad]
    kc_ref[0, 0] = jnp.concatenate(parts, axis=0).T.astype(BF16)
    vtc_ref[0, 0] = with_ones(_dot_nt(wuv_ref[...], ckv), MLA_HEADS, MLA_V).astype(BF16)


def _pre_mixer(h, g1, w_gu, w_down, gain, wnat, wtr, qn, kvn, wuq, wuk, wuv, tabt, layer, batch, seq):
    t = h.shape[0]
    nblk = seq // ATT_TILE
    resident = pl.Buffered(1)

    def row_blocked(n):
        return (jax.ShapeDtypeStruct((batch, nblk, ATT_TILE, n), BF16),
                pl.BlockSpec((1, 1, ATT_TILE, n), lambda b, i: (b, i, 0, 0)))

    def col_blocked(n):
        return (jax.ShapeDtypeStruct((batch, nblk, n, ATT_TILE), BF16),
                pl.BlockSpec((1, 1, n, ATT_TILE), lambda b, i: (b, i, 0, 0)))

    def q_t(n):
        return (jax.ShapeDtypeStruct((batch, n, seq), BF16),
                pl.BlockSpec((1, n, ATT_TILE), lambda b, i: (b, 0, i)))

    rows = pl.BlockSpec((ATT_TILE, D_MODEL), lambda b, i: (b * nblk + i, 0))
    outs = [(jax.ShapeDtypeStruct((t, D_MODEL), F32), rows),
            q_t(256), row_blocked(256), col_blocked(256),
            q_t(512), row_blocked(512), col_blocked(DIFF_HEADS * DIFF_VT),
            q_t(512), row_blocked(512), col_blocked(MLA_HEADS * MLA_VT)]
    whole = lambda shape: pl.BlockSpec((None,) + shape, lambda b, i: (layer,) + (0,) * len(shape),
                                       pipeline_mode=resident)
    return pl.pallas_call(
        _pre_mixer_kernel,
        grid=(batch, nblk),
        in_specs=[
            rows,
            whole((1, D_MODEL)),
            pl.BlockSpec((None, D_MODEL, D_FF), lambda b, i: (layer, 0, 0), pipeline_mode=resident),
            pl.BlockSpec((None, D_MODEL, D_FF), lambda b, i: (layer, 0, 1), pipeline_mode=resident),
            whole((D_FF, D_MODEL)),
            whole((1, D_MODEL)),
            whole((D_MODEL, N_NAT)),
            whole((N_TR, D_MODEL)),
            whole((1, MLA_Q_RANK)),
            whole((1, MLA_KV_RANK)),
            whole((4 * LANES, MLA_Q_RANK)),
            whole((MLA_HEADS * MLA_NOPE, MLA_KV_RANK)),
            whole((MLA_HEADS * MLA_V, MLA_KV_RANK)),
            pl.BlockSpec((1, 48, ATT_TILE), lambda b, i: (b, 0, i)),
        ],
        out_specs=[o[1] for o in outs],
        out_shape=[o[0] for o in outs],
        compiler_params=_params(("parallel", "parallel")),
        name="pre_mixer",
    )(h, g1, w_gu, w_gu, w_down, gain, wnat, wtr, qn, kvn, wuq, wuk, wuv, tabt)


def _row_mask(n_rows, lo, hi, n_cols):
    r = lax.broadcasted_iota(jnp.int32, (n_rows, n_cols), 0)
    return jnp.where((r >= lo) & (r < hi), 1.0, 0.0).astype(BF16)


def _emit_skewed(items):
    n_steps = max(k + len(stages) for k, stages in enumerate(items))
    for step in range(n_steps):
        for k, stages in reversed(list(enumerate(items))):
            s_idx = step - k
            if 0 <= s_idx < len(stages):
                stages[s_idx]()


def _softmax_items(k_blk, vt_blk, q_heads, kv_heads, m_ref, acc_ref, vt_rows, visible):
    items = []
    for hd in range(len(q_heads)):
        st = {}
        kv = kv_heads[hd]

        def scores(hd=hd, st=st, kv=kv):
            s = _dot(k_blk[:, kv * LANES:(kv + 1) * LANES], q_heads[hd])
            st["s"] = s if visible is None else jnp.where(visible, s, MASK_VALUE)

        def probs(hd=hd, st=st):
            s = st.pop("s")
            m_old = m_ref[hd]
            m_new = jnp.maximum(m_old, jnp.max(s, axis=0, keepdims=True))
            st["alpha"] = jnp.exp2(m_old - m_new)
            st["p"] = jnp.exp2(s - m_new).astype(BF16)
            m_ref[hd] = m_new

        def accumulate(hd=hd, st=st, kv=kv):
            pv = _dot(vt_blk[kv * vt_rows:(kv + 1) * vt_rows], st.pop("p"))
            acc_ref[hd] = st.pop("alpha") * acc_ref[hd] + pv

        items.append([scores, probs, accumulate])
    return items


def _shifted_softmax_items(k_blk, vt_blk, q_aug, kv_heads, acc_ref, vt_rows, ones_col, diag_masks):
    half = MXU_DIM
    items = []
    for hd in range(len(q_aug)):
        st = {}
        kv = kv_heads[hd]

        def scores(hd=hd, st=st, kv=kv):
            k_aug = jnp.concatenate([k_blk[:, kv * LANES:(kv + 1) * LANES], ones_col], axis=1)
            if diag_masks is None:
                st["s"] = (_dot(k_aug, q_aug[hd]),)
            else:
                wide, square = diag_masks
                st["s"] = (jnp.where(wide, _dot(k_aug[:half], q_aug[hd]), MASK_VALUE),
                           jnp.where(square, _dot(k_aug[half:], q_aug[hd][:, half:]), MASK_VALUE))

        def probs(st=st):
            st["p"] = tuple(jnp.exp2(s).astype(BF16) for s in st.pop("s"))

        def accumulate(hd=hd, st=st, kv=kv):
            vt = vt_blk[kv * vt_rows:(kv + 1) * vt_rows]
            p = st.pop("p")
            if diag_masks is None:
                acc_ref[hd] += _dot(vt, p[0])
            else:
                acc_ref[hd] = _dot(vt[:, :half], p[0])
                acc_ref[hd, :, half:] += _dot(vt[:, half:], p[1])

        items.append([scores, probs, accumulate])
    return items


def _log2_sigmoid_terms(y):
    neg_abs = pltpu.bitcast(pltpu.bitcast(y, jnp.uint32) | jnp.uint32(0x80000000), F32)
    log_beta = jnp.minimum(y, 0.0) - jnp.log2(1.0 + jnp.exp2(neg_abs))
    return log_beta, log_beta - y


def _stick_breaking_items(k_blk, vt_blk, q_heads, c_ref, acc_ref, upper):
    half = MXU_DIM
    items = []
    for hd in range(len(q_heads)):
        st = {}

        def logits(hd=hd, st=st):
            st["y"] = _dot(k_blk[:, (hd // 2) * LANES:(hd // 2 + 1) * LANES], q_heads[hd])

        def log_terms(st=st):
            log_beta, log_keep = _log2_sigmoid_terms(st.pop("y"))
            st.update(log_beta=log_beta, keep=log_keep.astype(BF16),
                      first=log_keep[0:1], mid=log_keep[half:half + 1])

        def suffix_sums(st=st):
            keep = st.pop("keep")
            st["later"] = _dot(upper, keep[half:])
            st["earlier"] = _dot(upper, keep[:half])

        def weights(hd=hd, st=st):
            later, earlier = st.pop("later"), st.pop("earlier")
            c_old = c_ref[hd]
            later_total = later[0:1] + st.pop("mid") + c_old
            between = jnp.concatenate([earlier + later_total, later + c_old], axis=0)
            st["a"] = jnp.exp2(st.pop("log_beta") + between).astype(BF16)
            c_ref[hd] = (earlier[0:1] + st.pop("first")) + later_total

        def accumulate(hd=hd, st=st):
            acc_ref[hd] += _dot(vt_blk[hd * SB_DIM:(hd + 1) * SB_DIM], st.pop("a"))

        items.append([logits, log_terms, suffix_sums, weights, accumulate])
    return items


def _stick_breaking_diag_items(k_blk, vt_blk, q_heads, c_ref, acc_ref, upper, masks):
    half = MXU_DIM
    items = []
    for hd in range(len(q_heads)):
        st = {}

        def logits(hd=hd, st=st):
            kh = k_blk[:, (hd // 2) * LANES:(hd // 2 + 1) * LANES]
            st["y"] = (_dot(kh[:half], q_heads[hd]), _dot(kh[half:], q_heads[hd][:, half:]))

        def log_terms(st=st):
            out = []
            for y, mask in zip(st.pop("y"), masks):
                log_beta, log_keep = _log2_sigmoid_terms(y)
                log_keep = jnp.where(mask, log_keep, 0.0)
                out.append((log_beta, log_keep.astype(BF16), log_keep[0:1]))
            st["terms"] = out

        def suffix_sums(st=st):
            st["sums"] = tuple(_dot(upper, keep) for _, keep, _ in st["terms"])

        def weights(hd=hd, st=st):
            earlier, later = st.pop("sums")
            (lb_e, _, first_e), (lb_l, _, first_l) = st.pop("terms")
            later_total = later[0:1] + first_l
            later_total = jnp.concatenate([jnp.zeros_like(later_total), later_total], axis=1)
            a_earlier = jnp.where(masks[0], jnp.exp2(lb_e + (earlier + later_total)), 0.0)
            a_later = jnp.where(masks[1], jnp.exp2(lb_l + later), 0.0)
            c_ref[hd] = (earlier[0:1] + first_e) + later_total
            st["a"] = (a_earlier.astype(BF16), a_later.astype(BF16))

        def accumulate(hd=hd, st=st):
            vt = vt_blk[hd * SB_DIM:(hd + 1) * SB_DIM]
            a_earlier, a_later = st.pop("a")
            acc_ref[hd] = _dot(vt[:, :half], a_earlier)
            acc_ref[hd, :, half:] += _dot(vt[:, half:], a_later)

        items.append([logits, log_terms, suffix_sums, weights, accumulate])
    return items


def _mixer_attn_kernel(lq1_ref, lk1_ref, lq2_ref, lk2_ref, subln_ref,
                       qta_ref, ka_ref, vta_ref, qtb_ref, kb_ref, vtb_ref, qtc_ref, kc_ref, vtc_ref,
                       *refs, lambda_init, shifted, nq):
    i = pl.program_id(1)
    ta = ATT_TILE
    if shifted:
        o_ref, worst_ref, c_ref, acc_a_ref, acc_b_ref, acc_c_ref, kmax_b_ref, kmax_c_ref = refs
    else:
        o_ref, c_ref, acc_a_ref, m_b_ref, acc_b_ref, m_c_ref, acc_c_ref = refs
        m_b_ref[...] = jnp.full_like(m_b_ref, MASK_VALUE)
        m_c_ref[...] = jnp.full_like(m_c_ref, MASK_VALUE)
        acc_b_ref[...] = jnp.zeros_like(acc_b_ref)
        acc_c_ref[...] = jnp.zeros_like(acc_c_ref)

    qta, qtb, qtc = qta_ref[0], qtb_ref[0], qtc_ref[0]
    q_a, q_b, q_c = [], [], []
    for hd in range(SB_HEADS):
        lo = (hd % 2) * SB_DIM
        q_a.append(qta[(hd // 2) * LANES:(hd // 2 + 1) * LANES] * _row_mask(LANES, lo, lo + SB_DIM, ta))
    for hd in range(DIFF_HEADS):
        qh = qtb[hd * LANES:(hd + 1) * LANES]
        q_b += [qh * _row_mask(LANES, 0, DIFF_QK, ta), qh * _row_mask(LANES, DIFF_QK, LANES, ta)]
    for hd in range(MLA_HEADS):
        q_c.append(qtc[hd * LANES:(hd + 1) * LANES])

    if shifted:
        @pl.when(i == 0)
        def _():
            for k_ref, kmax_ref in ((kb_ref, kmax_b_ref), (kc_ref, kmax_c_ref)):
                kmax = jnp.zeros((1, k_ref.shape[-1]), F32)
                for jb in range(nq):
                    kmax = jnp.maximum(kmax, jnp.max(jnp.abs(k_ref[0, jb].astype(F32)), axis=0, keepdims=True))
                kmax_ref[...] = kmax

        ones_col = jnp.where(lax.broadcasted_iota(jnp.int32, (ta, LANES), 1) == 0, 1.0, 0.0).astype(BF16)
        first_row = lax.broadcasted_iota(jnp.int32, (LANES, ta), 0) == 0
        first_of_16 = lax.broadcasted_iota(jnp.int32, (16, LANES), 0) == 0
        worst = jnp.zeros((1, 1), F32)

        def augment(q, kmax_row):
            kmax_tile = jnp.where(first_of_16, kmax_row, 0.0).astype(BF16)
            bound = _dot(kmax_tile, jnp.abs(q))[0:1]
            shift = bound * SHIFT_MARGIN
            extra = jnp.where(first_row, SHIFT_OFFSET - shift, 0.0).astype(BF16)
            return jnp.concatenate([q, extra], axis=0), jnp.max(shift, axis=1, keepdims=True)

        qb_aug, qc_aug = [], []
        for n, q in enumerate(q_b):
            kv = n // 2
            q_aug, top = augment(q, kmax_b_ref[:, kv * LANES:(kv + 1) * LANES])
            qb_aug.append(q_aug)
            worst = jnp.maximum(worst, top)
        for n, q in enumerate(q_c):
            q_aug, top = augment(q, kmax_c_ref[:, n * LANES:(n + 1) * LANES])
            qc_aug.append(q_aug)
            worst = jnp.maximum(worst, top)
        worst_ref[...] = jnp.broadcast_to(worst, worst_ref.shape)
    half = MXU_DIM
    kk = lax.broadcasted_iota(jnp.int32, (half, ta), 0)
    qq = lax.broadcasted_iota(jnp.int32, (half, ta), 1)
    kk_sq = lax.broadcasted_iota(jnp.int32, (half, half), 0)
    qq_sq = lax.broadcasted_iota(jnp.int32, (half, half), 1)
    visible = (kk <= qq, kk_sq <= qq_sq)
    strictly_before = (kk < qq, kk_sq < qq_sq)
    kk_full = lax.broadcasted_iota(jnp.int32, (ta, ta), 0)
    qq_full = lax.broadcasted_iota(jnp.int32, (ta, ta), 1)
    upper = jnp.where(qq_sq > kk_sq, 1.0, 0.0).astype(BF16)

    def block(j, diag):
        if diag:
            a_items = _stick_breaking_diag_items(ka_ref[0, j], vta_ref[0, j], q_a, c_ref, acc_a_ref, upper,
                                                 strictly_before)
        else:
            a_items = _stick_breaking_items(ka_ref[0, j], vta_ref[0, j], q_a, c_ref, acc_a_ref, upper)
        kv_b, kv_c = [n // 2 for n in range(2 * DIFF_HEADS)], list(range(MLA_HEADS))
        if shifted:
            masks = visible if diag else None
            b_items = _shifted_softmax_items(kb_ref[0, j], vtb_ref[0, j], qb_aug, kv_b, acc_b_ref, DIFF_VT,
                                             ones_col, masks)
            c_items = _shifted_softmax_items(kc_ref[0, j], vtc_ref[0, j], qc_aug, kv_c, acc_c_ref, MLA_VT,
                                             ones_col, masks)
        else:
            mask = (kk_full <= qq_full) if diag else None
            b_items = _softmax_items(kb_ref[0, j], vtb_ref[0, j], q_b, kv_b, m_b_ref, acc_b_ref, DIFF_VT, mask)
            c_items = _softmax_items(kc_ref[0, j], vtc_ref[0, j], q_c, kv_c, m_c_ref, acc_c_ref, MLA_VT, mask)
        items = []
        for hd in range(SB_HEADS):
            items += [b_items[2 * hd], a_items[hd], c_items[hd], b_items[2 * hd + 1]]
        _emit_skewed(items)

    block(i, True)

    def body(jj, carry):
        block(i - 1 - jj, False)
        return carry

    lax.fori_loop(0, i, body, 0)

    o_ref[0, :, 0:256] = acc_a_ref[...].reshape(SB_HEADS * SB_DIM, ta).T.astype(o_ref.dtype)

    lam = (jnp.exp(jnp.sum(lq1_ref[...] * lk1_ref[...], axis=-1, keepdims=True))
           - jnp.exp(jnp.sum(lq2_ref[...] * lk2_ref[...], axis=-1, keepdims=True)) + lambda_init)
    outs = []
    for hd in range(DIFF_HEADS):
        acc1, acc2 = acc_b_ref[2 * hd], acc_b_ref[2 * hd + 1]
        o = (acc1[0:DIFF_V] * (1.0 / acc1[DIFF_V:DIFF_V + 1])
             - acc2[0:DIFF_V] * (lam / acc2[DIFF_V:DIFF_V + 1]))
        ms = jnp.mean(o * o, axis=0, keepdims=True)
        outs.append(o * lax.rsqrt(ms + EPS))
    o_nat = jnp.concatenate(outs, axis=0).T
    o_ref[0, :, 256:768] = (o_nat * subln_ref[...] * (1.0 - lambda_init)).astype(o_ref.dtype)

    outs = []
    for hd in range(MLA_HEADS):
        acc = acc_c_ref[hd]
        outs.append(acc[0:MLA_V] * (1.0 / acc[MLA_V:MLA_V + 1]))
    o_ref[0, :, 768:1024] = jnp.concatenate(outs, axis=0).T.astype(o_ref.dtype)


def _mixer_attn(lq1, lk1, lq2, lk2, subln4, qta, ka, vta, qtb, kb, vtb, qtc, kc, vtc, layer, batch, seq):
    nq = seq // ATT_TILE
    lambda_init = 0.8 - 0.6 * math.exp(-0.3 * layer)
    vec = lambda n: pl.BlockSpec((None, 1, n), lambda b, i: (layer, 0, 0))
    q_t = lambda n: pl.BlockSpec((1, n, ATT_TILE), lambda b, i: (b, 0, i))
    per_batch = lambda r, c: pl.BlockSpec((1, nq, r, c), lambda b, i: (b, 0, 0, 0),
                                          pipeline_mode=pl.Buffered(1))
    operands = (lq1, lk1, lq2, lk2, subln4, qta, ka, vta, qtb, kb, vtb, qtc, kc, vtc)
    in_specs = [
        vec(DIFF_QK), vec(DIFF_QK), vec(DIFF_QK), vec(DIFF_QK), vec(DIFF_HEADS * DIFF_V),
        q_t(256), per_batch(ATT_TILE, 256), per_batch(256, ATT_TILE),
        q_t(512), per_batch(ATT_TILE, 512), per_batch(DIFF_HEADS * DIFF_VT, ATT_TILE),
        q_t(512), per_batch(ATT_TILE, 512), per_batch(MLA_HEADS * MLA_VT, ATT_TILE),
    ]
    mixed_spec = pl.BlockSpec((1, ATT_TILE, D_MODEL), lambda b, i: (b, i, 0))
    mixed_shape = jax.ShapeDtypeStruct((batch, seq, D_MODEL), BF16)
    state = lambda n, rows: pltpu.VMEM((n, rows, ATT_TILE), F32)

    def call(shifted):
        if shifted:
            out_specs = [mixed_spec, pl.BlockSpec((1, 1, 8, LANES), lambda b, i: (b, i, 0, 0))]
            out_shape = [mixed_shape, jax.ShapeDtypeStruct((batch, nq, 8, LANES), F32)]
            scratch = [state(SB_HEADS, 1), state(SB_HEADS, SB_DIM), state(2 * DIFF_HEADS, DIFF_VT),
                       state(MLA_HEADS, MLA_VT), pltpu.VMEM((1, 512), F32), pltpu.VMEM((1, 512), F32)]
        else:
            out_specs, out_shape = mixed_spec, mixed_shape
            scratch = [state(SB_HEADS, 1), state(SB_HEADS, SB_DIM), state(2 * DIFF_HEADS, 1),
                       state(2 * DIFF_HEADS, DIFF_VT), state(MLA_HEADS, 1), state(MLA_HEADS, MLA_VT)]
        return pl.pallas_call(
            functools.partial(_mixer_attn_kernel, lambda_init=lambda_init, shifted=shifted, nq=nq),
            grid=(batch, nq),
            in_specs=in_specs,
            out_specs=out_specs,
            out_shape=out_shape,
            scratch_shapes=scratch,
            compiler_params=_params(("arbitrary", "arbitrary")),
            name="mixer_attn" if shifted else "mixer_attn_online",
        )(*operands)

    mixed, worst = call(True)
    return lax.cond(jnp.max(worst) > SHIFT_LIMIT, lambda: call(False), lambda: mixed)


def _post_mixer_kernel(h_ref, mix_ref, p_ref, wo_ref, g2_ref, wg_ref, wv_ref, wd_ref,
                       gp_ref, wgate_ref, wpp_ref, fin_ref, o_ref, *, final):
    h = h_ref[...] + _dot(mix_ref[...], wo_ref[...])
    h = _half_step_ffn(h, g2_ref[...], wg_ref, wv_ref, wd_ref)
    gate = jax.nn.sigmoid(_dot(_rms_norm(h, gp_ref[...]).astype(BF16), wgate_ref[...]))
    out = h + _dot(p_ref[0].astype(BF16), wpp_ref[...]) * gate
    if final:
        out = _rms_norm(out, fin_ref[...])
    o_ref[...] = out


def _post_mixer(h, mixed, p, w_out, g2, w_gu, w_down, g_ple, w_gate, w_pproj, norm_final, layer, final):
    t = h.shape[0]
    resident = pl.Buffered(1)
    whole = lambda shape: pl.BlockSpec((None,) + shape, lambda i: (layer,) + (0,) * len(shape),
                                       pipeline_mode=resident)
    rows = pl.BlockSpec((ROW_TILE, D_MODEL), lambda i: (i, 0))
    return pl.pallas_call(
        functools.partial(_post_mixer_kernel, final=final),
        grid=(t // ROW_TILE,),
        in_specs=[
            rows, rows,
            pl.BlockSpec((1, ROW_TILE, PLE_DIM), lambda i: (layer, i, 0)),
            whole((D_MODEL, D_MODEL)),
            whole((1, D_MODEL)),
            pl.BlockSpec((None, D_MODEL, D_FF), lambda i: (layer, 0, 0), pipeline_mode=resident),
            pl.BlockSpec((None, D_MODEL, D_FF), lambda i: (layer, 0, 1), pipeline_mode=resident),
            whole((D_FF, D_MODEL)),
            whole((1, D_MODEL)),
            whole((D_MODEL, D_MODEL)),
            whole((PLE_DIM, D_MODEL)),
            pl.BlockSpec((1, D_MODEL), lambda i: (0, 0)),
        ],
        out_specs=rows,
        out_shape=jax.ShapeDtypeStruct((t, D_MODEL), F32),
        compiler_params=_params(("parallel",)),
        name="post_mixer",
    )(h, mixed, p, w_out, g2, w_gu, w_gu, w_down, g_ple, w_gate, w_pproj, norm_final)


def _rope_tables(positions):
    pos = positions.astype(F32)[..., None]

    def cs(rot_dim):
        inv_freq = 1.0 / (ROPE_THETA ** (jnp.arange(0, rot_dim, 2, dtype=F32) / rot_dim))
        ang = pos * inv_freq
        return jnp.cos(ang), jnp.sin(ang)

    cos_d, sin_d = cs(2 * DIFF_ROT_HALF)
    cos_m, sin_m = cs(2 * MLA_ROT_HALF)
    return jnp.swapaxes(jnp.concatenate([cos_d, sin_d, cos_m, sin_m], -1), 1, 2)


def _prep_mixer_weights(w_in, mla_w_uq, mla_w_ukv):
    n_l = w_in.shape[0]
    seg = lambda i: w_in[:, :, _OFFS[i]:_OFFS[i + 1]]
    a_q, a_k, a_v, b_q, b_k, b_v, c_q, c_kv, k_pe = (seg(i) for i in range(9))
    wnat = jnp.concatenate([a_k, c_q, c_kv], -1).astype(BF16)
    wtr = jnp.swapaxes(jnp.concatenate([a_q, a_v, b_q, b_v, b_k, k_pe], -1), 1, 2).astype(BF16)
    uq = mla_w_uq.reshape(n_l, MLA_Q_RANK, MLA_HEADS, MLA_NOPE + MLA_ROPE)
    uq = jnp.pad(uq, ((0, 0), (0, 0), (0, 0), (0, LANES - MLA_NOPE - MLA_ROPE)))
    wuq = jnp.swapaxes(uq.reshape(n_l, MLA_Q_RANK, MLA_HEADS * LANES), 1, 2).astype(BF16)
    ukv = mla_w_ukv.reshape(n_l, MLA_KV_RANK, MLA_HEADS, MLA_NOPE + MLA_V)
    wuk = jnp.swapaxes(ukv[..., :MLA_NOPE].reshape(n_l, MLA_KV_RANK, MLA_HEADS * MLA_NOPE), 1, 2).astype(BF16)
    wuv = jnp.swapaxes(ukv[..., MLA_NOPE:].reshape(n_l, MLA_KV_RANK, MLA_HEADS * MLA_V), 1, 2).astype(BF16)
    return wnat, wtr, wuq, wuk, wuv


def kernel(x, p, positions, norm_ffn1, w_ffn1_gu, w_ffn1_down, norm_mix, w_in, mla_q_norm, mla_w_uq,
           mla_kv_norm, mla_w_ukv, diff_lambda_q1, diff_lambda_k1, diff_lambda_q2, diff_lambda_k2,
           diff_subln, w_out, norm_ffn2, w_ffn2_gu, w_ffn2_down, norm_ple, w_ple_gate, w_ple_proj,
           norm_final):
    batch, seq, _ = x.shape
    depth = w_in.shape[0]
    assert seq % ATT_TILE == 0 and ATT_TILE == ROW_TILE
    t = batch * seq
    row3 = lambda a: a.reshape(a.shape[0], 1, a.shape[1])

    tabt = _rope_tables(positions)
    wnat, wtr, wuq, wuk, wuv = _prep_mixer_weights(w_in, mla_w_uq, mla_w_ukv)
    w1_gu, w1_d = w_ffn1_gu.astype(BF16), w_ffn1_down.astype(BF16)
    w2_gu, w2_d = w_ffn2_gu.astype(BF16), w_ffn2_down.astype(BF16)
    w_out_b = w_out.astype(BF16)
    w_gate_b, w_pproj_b = w_ple_gate.astype(BF16), w_ple_proj.astype(BF16)
    g_ffn1, g_mix, g_ffn2, g_ple = row3(norm_ffn1), row3(norm_mix), row3(norm_ffn2), row3(norm_ple)
    g_q, g_kv = row3(mla_q_norm), row3(mla_kv_norm)
    lq1, lk1, lq2, lk2 = (row3(a) for a in (diff_lambda_q1, diff_lambda_k1, diff_lambda_q2, diff_lambda_k2))
    subln4 = row3(jnp.tile(diff_subln, (1, DIFF_HEADS)))
    p_rows = p.reshape(depth, t, PLE_DIM)
    fin = norm_final.reshape(1, D_MODEL)

    h = x.reshape(t, D_MODEL)
    for layer in range(depth):
        h, qta, ka, vta, qtb, kb, vtb, qtc, kc, vtc = _pre_mixer(
            h, g_ffn1, w1_gu, w1_d, g_mix, wnat, wtr, g_q, g_kv, wuq, wuk, wuv, tabt, layer, batch, seq)
        mixed = _mixer_attn(lq1, lk1, lq2, lk2, subln4, qta, ka, vta, qtb, kb, vtb, qtc, kc, vtc,
                            layer, batch, seq)
        h = _post_mixer(h, mixed.reshape(t, D_MODEL), p_rows, w_out_b, g_ffn2, w2_gu, w2_d, g_ple,
                        w_gate_b, w_pproj_b, fin, layer, final=(layer == depth - 1))
    return h.reshape(batch, seq, D_MODEL)
```
